```python
import jax, jax.numpy as jnp
from jax import lax
import numpy as np

D_MODEL = 1024
BATCH = 4
SEQ = 4096
DEPTH = 2

CHUNK = 64
D_MIX = D_MODEL
NH_M = 4
DH_M = 128
D_M = NH_M * DH_M
D_R = D_MIX - D_M
NB_R = 8
DB_R = D_R // NB_R
CONV_W = 4
LRU_C = 8.0
D_FF = 2816
N_EXPERTS = 8
TOP_K = 2
D_FF_E = 2816
MOE_BLOCK = 128
D_PLE = 256
EPS = 1e-6
N_DENSE = (DEPTH + 1) // 2
N_MOE = DEPTH // 2
Q_END = D_M
K_END = 2 * D_M
V_END = 3 * D_M
O_END = 4 * D_M
I_END = O_END + NH_M
F_END = I_END + NH_M
XR_END = F_END + D_R
D_IN = XR_END + D_R

kernel_name = "hymba_mlstm_rglru_moe_ple"


def group_rmsnorm(x, g, n_groups):
    xf = x.astype(jnp.float32)
    xg = xf.reshape(*x.shape[:-1], n_groups, x.shape[-1] // n_groups)
    xg = xg * lax.rsqrt(jnp.mean(xg * xg, axis=-1, keepdims=True) + EPS)
    return (xg.reshape(x.shape) * g.astype(jnp.float32)).astype(x.dtype)


def rmsnorm(x, g):
    return group_rmsnorm(x, g, 1)


def causal_conv(x, w, b):
    S = x.shape[1]
    xp = jnp.pad(x, ((0, 0), (CONV_W - 1, 0), (0, 0)))
    return sum(xp[:, j:j + S] * w[j] for j in range(CONV_W)) + b


def mlstm_chunkwise(q, k, v, i_pre, f_pre):
    B, S, H, Dh = q.shape
    L = CHUNK
    NC = S // L
    f32 = jnp.float32

    def blk(t):
        t = t.astype(f32).reshape(B, NC, L, *t.shape[2:])
        return jnp.moveaxis(t, 2, 3)

    qc = blk(q) * (Dh ** -0.5)
    kc, vc = blk(k), blk(v)
    ig = blk(i_pre)
    b = jnp.cumsum(blk(jax.nn.log_sigmoid(f_pre.astype(f32))), axis=-1)
    b_last = b[..., -1]
    a = b_last[..., None] - b + ig

    def step(carry, xs):
        C, n, m = carry
        bl, a_c, k_c, v_c = xs
        m_new = jnp.maximum(bl + m, a_c.max(-1))
        decay = jnp.exp(bl + m - m_new)
        wk = jnp.exp(a_c - m_new[..., None])
        C_new = decay[..., None, None] * C + jnp.einsum('bhl,bhld,bhle->bhde', wk, k_c, v_c)
        n_new = decay[..., None] * n + jnp.einsum('bhl,bhld->bhd', wk, k_c)
        return (C_new, n_new, m_new), (C, n, m)

    init = (jnp.zeros((B, H, Dh, Dh), f32), jnp.zeros((B, H, Dh), f32), jnp.zeros((B, H), f32))
    xs = (jnp.moveaxis(b_last, 1, 0), jnp.moveaxis(a, 1, 0),
          jnp.moveaxis(kc, 1, 0), jnp.moveaxis(vc, 1, 0))
    _, (C_prev, n_prev, m_prev) = lax.scan(step, init, xs)
    C_prev = jnp.moveaxis(C_prev, 0, 1)
    n_prev = jnp.moveaxis(n_prev, 0, 1)
    m_prev = jnp.moveaxis(m_prev, 0, 1)

    causal = jnp.tril(jnp.ones((L, L), dtype=bool))
    logD = jnp.where(causal, b[..., :, None] - b[..., None, :] + ig[..., None, :], -jnp.inf)
    m_inter = b + m_prev[..., None]
    m_row = jnp.maximum(m_inter, logD.max(-1))
    s = jnp.einsum('bchld,bchmd->bchlm', qc, kc) * jnp.exp(logD - m_row[..., None])
    inter = jnp.exp(m_inter - m_row)
    num = (jnp.einsum('bchlm,bchme->bchle', s, vc)
           + inter[..., None] * jnp.einsum('bchld,bchde->bchle', qc, C_prev))
    den = s.sum(-1) + inter * jnp.einsum('bchld,bchd->bchl', qc, n_prev)
    h = num / jnp.maximum(jnp.abs(den), jnp.exp(-m_row))[..., None]
    h = jnp.moveaxis(h, 3, 2).reshape(B, S, H, Dh)
    return h.astype(q.dtype)


def rglru(x, w_a, b_a, w_x, b_x, lam):
    B, S, _ = x.shape
    xf = x.astype(jnp.float32)
    xb = xf.reshape(B, S, NB_R, DB_R)
    r = jax.nn.sigmoid(jnp.einsum('bsnd,nde->bsne', xb, w_a.astype(jnp.float32)).reshape(B, S, D_R) + b_a)
    i = jax.nn.sigmoid(jnp.einsum('bsnd,nde->bsne', xb, w_x.astype(jnp.float32)).reshape(B, S, D_R) + b_x)
    log_a = -LRU_C * r * jax.nn.softplus(-lam.astype(jnp.float32))
    a = jnp.exp(log_a)
    u = jnp.sqrt(-jnp.expm1(2.0 * log_a)) * (i * xf)

    def combine(left, right):
        a_l, u_l = left
        a_r, u_r = right
        return a_l * a_r, a_r * u_l + u_r

    _, h = lax.associative_scan(combine, (a, u), axis=1)
    return h.astype(x.dtype)


def hybrid_mixer(hn, w_in, b_in, w_conv_qk, b_conv_qk, g_mh, w_conv_r, b_conv_r,
                 w_ra, b_ra, w_ri, b_ri, lam, g_r, w_out):
    B, S, _ = hn.shape
    z = hn @ w_in + b_in
    qk = jax.nn.silu(causal_conv(z[..., :K_END], w_conv_qk, b_conv_qk))
    q = qk[..., :D_M].reshape(B, S, NH_M, DH_M)
    k = qk[..., D_M:].reshape(B, S, NH_M, DH_M)
    v = z[..., K_END:V_END].reshape(B, S, NH_M, DH_M)
    o = jax.nn.sigmoid(z[..., V_END:O_END])
    hm = mlstm_chunkwise(q, k, v, z[..., O_END:I_END], z[..., I_END:F_END]).reshape(B, S, D_M) * o
    hm = group_rmsnorm(hm, g_mh, NH_M)
    xr = causal_conv(z[..., F_END:XR_END], w_conv_r, b_conv_r)
    hr = rglru(xr, w_ra, b_ra, w_ri, b_ri, lam) * jax.nn.gelu(z[..., XR_END:], approximate=True)
    hr = group_rmsnorm(hr, g_r, NB_R)
    return jnp.concatenate([hm, hr], axis=-1) @ w_out


def swiglu(x, w_gate, w_up, w_down):
    return (jax.nn.silu(x @ w_gate) * (x @ w_up)) @ w_down


def moe_swiglu(x, w_router, w_gate, w_up, w_down):
    B, S, D = x.shape
    N = B * S
    A = N * TOP_K
    xt = x.reshape(N, D)
    logits = xt.astype(jnp.float32) @ w_router.astype(jnp.float32)
    top_val, top_idx = lax.top_k(logits, TOP_K)
    gates = jax.nn.softmax(top_val, axis=-1)
    e_flat = top_idx.reshape(A)
    tok_flat = jnp.repeat(jnp.arange(N, dtype=jnp.int32), TOP_K)
    g_flat = gates.reshape(A)
    order = jnp.argsort(e_flat)
    e_sorted, tok_sorted, g_sorted = e_flat[order], tok_flat[order], g_flat[order]
    counts = jnp.bincount(e_flat, length=N_EXPERTS)
    padded = (counts + MOE_BLOCK - 1) // MOE_BLOCK * MOE_BLOCK
    p_end = jnp.cumsum(padded)
    p_start = p_end - padded
    u_start = jnp.cumsum(counts) - counts
    dest = p_start[e_sorted] + jnp.arange(A, dtype=jnp.int32) - u_start[e_sorted]
    P = A + N_EXPERTS * MOE_BLOCK
    n_blocks = P // MOE_BLOCK
    row_tok = jnp.full((P,), N, dtype=jnp.int32).at[dest].set(tok_sorted)
    row_gate = jnp.zeros((P,), jnp.float32).at[dest].set(g_sorted)
    block_expert = jnp.minimum(
        jnp.searchsorted(p_end, jnp.arange(n_blocks, dtype=jnp.int32) * MOE_BLOCK, side='right'),
        N_EXPERTS - 1)
    xpad = jnp.concatenate([xt, jnp.zeros((1, D), xt.dtype)], axis=0)

    def expert_block(args):
        toks, e = args
        xb = xpad[toks]
        return (jax.nn.silu(xb @ w_gate[e]) * (xb @ w_up[e])) @ w_down[e]

    y_rows = lax.map(expert_block, (row_tok.reshape(n_blocks, MOE_BLOCK), block_expert)).reshape(P, D)
    out = jnp.zeros((N + 1, D), jnp.float32).at[row_tok].add(y_rows.astype(jnp.float32) * row_gate[:, None])
    return out[:N].reshape(B, S, D).astype(x.dtype)


def setup_inputs(seed: int = 0) -> dict:
    key = jax.random.key(seed)
    ks = iter(jax.random.split(key, 40))
    nrm = lambda shape, scale: jax.random.normal(next(ks), shape, jnp.float32) * scale
    gain = lambda shape: 1.0 + nrm(shape, 0.02)
    a0 = jax.random.uniform(next(ks), (DEPTH, D_R), jnp.float32, 0.9, 0.999)
    s = a0 ** (1.0 / LRU_C)
    lam = jnp.log(s) - jnp.log1p(-s)
    f_bias = 3.0 + 3.0 * jax.random.uniform(next(ks), (DEPTH, NH_M), jnp.float32)
    b_in = nrm((DEPTH, D_IN), 0.02).at[:, I_END:F_END].set(f_bias)
    return {
        "x": nrm((BATCH, SEQ, D_MODEL), 1.0),
        "p": nrm((DEPTH, BATCH, SEQ, D_PLE), 1.0),
        "g_mix": gain((DEPTH, D_MODEL)),
        "w_in": nrm((DEPTH, D_MODEL, D_IN), D_MODEL ** -0.5),
        "b_in": b_in,
        "w_conv_qk": nrm((DEPTH, CONV_W, 2 * D_M), CONV_W ** -0.5),
        "b_conv_qk": nrm((DEPTH, 2 * D_M), 0.02),
        "g_mh": gain((DEPTH, D_M)),
        "w_conv_r": nrm((DEPTH, CONV_W, D_R), CONV_W ** -0.5),
        "b_conv_r": nrm((DEPTH, D_R), 0.02),
        "w_ra": nrm((DEPTH, NB_R, DB_R, DB_R), DB_R ** -0.5),
        "b_ra": nrm((DEPTH, D_R), 0.02),
        "w_ri": nrm((DEPTH, NB_R, DB_R, DB_R), DB_R ** -0.5),
        "b_ri": nrm((DEPTH, D_R), 0.02),
        "lam": lam,
        "g_r": gain((DEPTH, D_R)),
        "w_out": nrm((DEPTH, D_MIX, D_MODEL), D_MIX ** -0.5),
        "g_ffn": gain((DEPTH, D_MODEL)),
        "w_ff_gate": nrm((N_DENSE, D_MODEL, D_FF), D_MODEL ** -0.5),
        "w_ff_up": nrm((N_DENSE, D_MODEL, D_FF), D_MODEL ** -0.5),
        "w_ff_down": nrm((N_DENSE, D_FF, D_MODEL), D_FF ** -0.5),
        "w_router": nrm((N_MOE, D_MODEL, N_EXPERTS), D_MODEL ** -0.5),
        "w_e_gate": nrm((N_MOE, N_EXPERTS, D_MODEL, D_FF_E), D_MODEL ** -0.5),
        "w_e_up": nrm((N_MOE, N_EXPERTS, D_MODEL, D_FF_E), D_MODEL ** -0.5),
        "w_e_down": nrm((N_MOE, N_EXPERTS, D_FF_E, D_MODEL), D_FF_E ** -0.5),
        "g_ple": gain((DEPTH, D_MODEL)),
        "w_ple_gate": nrm((DEPTH, D_MODEL, D_MODEL), D_MODEL ** -0.5),
        "w_ple_proj": nrm((DEPTH, D_PLE, D_MODEL), D_PLE ** -0.5),
        "g_final": gain((D_MODEL,)),
    }


def reference(x, p, g_mix, w_in, b_in, w_conv_qk, b_conv_qk, g_mh, w_conv_r, b_conv_r,
              w_ra, b_ra, w_ri, b_ri, lam, g_r, w_out, g_ffn, w_ff_gate, w_ff_up, w_ff_down,
              w_router, w_e_gate, w_e_up, w_e_down, g_ple, w_ple_gate, w_ple_proj, g_final):
    h = x
    for i in range(DEPTH):
        hn = rmsnorm(h, g_mix[i])
        h = h + hybrid_mixer(hn, w_in[i], b_in[i], w_conv_qk[i], b_conv_qk[i], g_mh[i],
                             w_conv_r[i], b_conv_r[i], w_ra[i], b_ra[i], w_ri[i], b_ri[i],
                             lam[i], g_r[i], w_out[i])
        hn = rmsnorm(h, g_ffn[i])
        j = i // 2
        if i % 2 == 0:
            h = h + swiglu(hn, w_ff_gate[j], w_ff_up[j], w_ff_down[j])
        else:
            h = h + moe_swiglu(hn, w_router[j], w_e_gate[j], w_e_up[j], w_e_down[j])
        gate = jax.nn.sigmoid(rmsnorm(h, g_ple[i]) @ w_ple_gate[i])
        h = h + gate * (p[i] @ w_ple_proj[i])
    return rmsnorm(h, g_final)
```

```python
import functools

import jax
import jax.numpy as jnp
from jax import lax
from jax.experimental import pallas as pl
from jax.experimental.pallas import tpu as pltpu

F32 = jnp.float32
BF16 = jnp.bfloat16

D_MODEL = 1024
NH_M = 4
DH_M = 128
D_M = NH_M * DH_M
D_R = D_MODEL - D_M
NB_R = 8
DB_R = D_R // NB_R
CONV_W = 4
LRU_C = 8.0
D_FF = 2816
N_EXPERTS = 8
TOP_K = 2
D_PLE = 256
EPS = 1e-6

LANES = 128
SUBLANES = 8
V7X_VMEM_BYTES = 64 * 1024 * 1024

ROW_TILE = 512
MIX_CHUNK = 128
FFN_ROW_TILE = 512
FFN_COL_CHUNK = 256


def _vmem_limit(nbytes):
    return int(min(nbytes, V7X_VMEM_BYTES - 6 * 1024 * 1024))


def _rmsnorm(x, g):
    return x * lax.rsqrt(jnp.mean(x * x, axis=-1, keepdims=True) + EPS) * g


def _sigmoid(x):
    return 1.0 / (1.0 + jnp.exp(-x))


def _silu(x):
    return x * _sigmoid(x)


def _inproj_kernel(x_ref, g_ref, w_ref, b_ref, wg_ref, bg_ref,
                   zqk_ref, v_ref, o_ref, zxr_ref, yr_ref, gt_ref):
    xn = _rmsnorm(x_ref[...], g_ref[...]).astype(BF16)

    def seg(lo, hi):
        return jnp.dot(xn, w_ref[:, lo:hi], preferred_element_type=F32) + b_ref[:, lo:hi]

    zqk_ref[...] = seg(0, 2 * D_M)
    v_ref[...] = seg(2 * D_M, 3 * D_M).astype(BF16)
    o_ref[...] = seg(3 * D_M, 4 * D_M)
    zxr_ref[...] = seg(4 * D_M, 4 * D_M + D_R)
    yr_ref[...] = seg(4 * D_M + D_R, 4 * D_M + 2 * D_R)
    gt_ref[...] = lax.dot_general(wg_ref[...], xn, (((1,), (1,)), ((), ())),
                                  preferred_element_type=F32) + bg_ref[...]


def _norm_inproj(h, g, w_main, b_main, w_gt, b_gt):
    n = h.shape[0]
    tm = ROW_TILE
    wcols = w_main.shape[1]
    row = lambda i: (i, 0)
    const = lambda i: (0, 0)
    return pl.pallas_call(
        _inproj_kernel,
        grid=(n // tm,),
        in_specs=[
            pl.BlockSpec((tm, D_MODEL), row),
            pl.BlockSpec((1, D_MODEL), const),
            pl.BlockSpec((D_MODEL, wcols), const),
            pl.BlockSpec((1, wcols), const),
            pl.BlockSpec((2 * NH_M, D_MODEL), const),
            pl.BlockSpec((2 * NH_M, 1), const),
        ],
        out_specs=[
            pl.BlockSpec((tm, 2 * D_M), row),
            pl.BlockSpec((tm, D_M), row),
            pl.BlockSpec((tm, D_M), row),
            pl.BlockSpec((tm, D_R), row),
            pl.BlockSpec((tm, D_R), row),
            pl.BlockSpec((2 * NH_M, tm), lambda i: (0, i)),
        ],
        out_shape=[
            jax.ShapeDtypeStruct((n, 2 * D_M), F32),
            jax.ShapeDtypeStruct((n, D_M), BF16),
            jax.ShapeDtypeStruct((n, D_M), F32),
            jax.ShapeDtypeStruct((n, D_R), F32),
            jax.ShapeDtypeStruct((n, D_R), F32),
            jax.ShapeDtypeStruct((2 * NH_M, n), F32),
        ],
        compiler_params=pltpu.CompilerParams(
            dimension_semantics=("arbitrary",), vmem_limit_bytes=_vmem_limit(48 << 20)),
        name="norm_inproj",
    )(h, g, w_main, b_main, w_gt, b_gt)


def _causal_conv(x, prev, w, b):
    t = x.shape[0]
    xe = jnp.concatenate([prev, x], axis=0)
    acc = xe[SUBLANES:] * w[CONV_W - 1:CONV_W, :] + b
    for j in range(CONV_W - 1):
        shifted = pltpu.roll(xe, CONV_W - 1 - j, axis=0)[SUBLANES:]
        acc = acc + shifted * w[j:j + 1, :]
    del t
    return acc


def _mixer_kernel(zqk_ref, v_ref, o_ref, zxr_ref, yr_ref, gt_ref,
                  wcqk_ref, bcqk_ref, gmh_ref, wcr_ref, bcr_ref, wbd_ref, bbd_ref,
                  lam_ref, gr_ref, ones_ref,
                  out_ref,
                  pqk_ref, pxr_ref, s_ref, m_ref, hr_ref):
    tm = zqk_ref.shape[0]
    first = pl.program_id(1) == 0

    @pl.when(first)
    def _():
        pqk_ref[...] = jnp.zeros_like(pqk_ref)
        pxr_ref[...] = jnp.zeros_like(pxr_ref)
        s_ref[...] = jnp.zeros_like(s_ref)
        m_ref[...] = jnp.zeros_like(m_ref)
        hr_ref[...] = jnp.zeros_like(hr_ref)

    zqk = zqk_ref[...]
    qk = _silu(_causal_conv(zqk, pqk_ref[...], wcqk_ref[...], bcqk_ref[...]))
    pqk_ref[...] = zqk[tm - SUBLANES:, :]

    gt = gt_ref[...]
    logf = jnp.minimum(gt, 0.0) - jnp.log(1.0 + jnp.exp(-jnp.abs(gt)))
    L = MIX_CHUNK
    ri = lax.broadcasted_iota(jnp.int32, (L, L), 0)
    ci = lax.broadcasted_iota(jnp.int32, (L, L), 1)
    tril = ci <= ri
    upper = (ri <= ci).astype(F32)
    ones_blk = jnp.ones((L, DH_M), BF16)
    scale = DH_M ** -0.5
    neg_inf = -jnp.inf

    b_all = jnp.concatenate(
        [jnp.dot(logf[:, c * L:(c + 1) * L], upper, preferred_element_type=F32,
                 precision=lax.Precision.HIGHEST) for c in range(tm // L)], axis=1)

    for h in range(NH_M):
        s_state = s_ref[h]
        m_prev = m_ref[h]
        for c in range(tm // L):
            r0 = c * L
            q = (qk[r0:r0 + L, h * DH_M:(h + 1) * DH_M] * scale).astype(BF16)
            k32 = qk[r0:r0 + L, D_M + h * DH_M:D_M + (h + 1) * DH_M]
            kt32 = k32.T
            v_aug = jnp.concatenate([v_ref[r0:r0 + L, h * DH_M:(h + 1) * DH_M], ones_blk], axis=1)
            ig_row = gt[h:h + 1, r0:r0 + L]
            lf_row = logf[NH_M + h:NH_M + h + 1, r0:r0 + L]
            b_row = b_all[NH_M + h:NH_M + h + 1, r0:r0 + L]
            c_row = ig_row - b_row
            g_col = jnp.maximum(jnp.max(jnp.where(tril, c_row, neg_inf), axis=-1, keepdims=True), m_prev)
            b_col = jnp.sum(jnp.where(tril, lf_row, 0.0), axis=-1, keepdims=True)
            g_last = g_col[L - 1:L, :]
            b_last = b_col[L - 1:L, :]
            dmat = jnp.exp(jnp.where(tril, c_row - g_col, neg_inf))
            s = jnp.dot(q, kt32.astype(BF16), preferred_element_type=F32) * dmat
            inter = jnp.exp(m_prev - g_col)
            out = (jnp.dot(s.astype(BF16), v_aug, preferred_element_type=F32)
                   + inter * jnp.dot(q, s_state.astype(BF16), preferred_element_type=F32))
            num = out[:, :DH_M]
            den = out[:, DH_M:]
            hm = num / jnp.maximum(jnp.abs(den), jnp.exp(-(b_col + g_col)))
            hm = hm * _sigmoid(o_ref[r0:r0 + L, h * DH_M:(h + 1) * DH_M])
            hm = hm * lax.rsqrt(jnp.mean(hm * hm, axis=-1, keepdims=True) + EPS)
            out_ref[r0:r0 + L, h * DH_M:(h + 1) * DH_M] = (
                hm * gmh_ref[:, h * DH_M:(h + 1) * DH_M]).astype(out_ref.dtype)
            wk_row = jnp.exp(c_row - g_last)
            decay = jnp.exp(m_prev - g_last)
            s_state = decay * s_state + jnp.dot((kt32 * wk_row).astype(BF16), v_aug,
                                                preferred_element_type=F32)
            m_prev = b_last + g_last
        s_ref[h] = s_state
        m_ref[h] = m_prev

    zxr = zxr_ref[...]
    xr = _causal_conv(zxr, pxr_ref[...], wcr_ref[...], bcr_ref[...])
    pxr_ref[...] = zxr[tm - SUBLANES:, :]
    gates = _sigmoid(jnp.dot(xr.astype(BF16), wbd_ref[...], preferred_element_type=F32) + bbd_ref[...])
    r_gate = gates[:, :D_R]
    i_gate = gates[:, D_R:]
    nlam = -lam_ref[...]
    softplus = jnp.maximum(nlam, 0.0) + jnp.log(1.0 + jnp.exp(-jnp.abs(nlam)))
    log_a = (-LRU_C) * r_gate * softplus
    a = jnp.exp(log_a)
    u = jnp.sqrt(1.0 - jnp.exp(2.0 * log_a)) * (i_gate * xr)
    row8 = lax.broadcasted_iota(jnp.int32, (tm, D_R), 0) & (SUBLANES - 1)
    for sft in (1, 2, 4):
        keep = row8 >= sft
        a_sh = pltpu.roll(a, sft, axis=0)
        u_sh = pltpu.roll(u, sft, axis=0)
        u = jnp.where(keep, a * u_sh + u, u)
        a = jnp.where(keep, a * a_sh, a)
    hprev = hr_ref[...]
    rows = []
    for gi in range(tm // SUBLANES):
        blk = u[gi * SUBLANES:(gi + 1) * SUBLANES] + a[gi * SUBLANES:(gi + 1) * SUBLANES] * hprev
        rows.append(blk)
        hprev = blk[SUBLANES - 1:SUBLANES, :]
    hr_ref[...] = hprev
    hseq = jnp.concatenate(rows, axis=0)
    yr = yr_ref[...]
    gelu = 0.5 * yr * (1.0 + jnp.tanh(0.7978845608028654 * (yr + 0.044715 * yr * yr * yr)))
    hr = hseq * gelu
    sq = hr * hr
    sq_hi = sq.astype(BF16)
    sq_lo = (sq - sq_hi.astype(F32)).astype(BF16)
    gsum = (jnp.dot(sq_hi, ones_ref[...], preferred_element_type=F32)
            + jnp.dot(sq_lo, ones_ref[...], preferred_element_type=F32))
    hr = hr * lax.rsqrt(gsum * (1.0 / DB_R) + EPS) * gr_ref[...]
    out_ref[:, D_M:] = hr.astype(out_ref.dtype)


def _mixer(zqk, v, o, zxr, yr, gt, wcqk, bcqk, gmh, wcr, bcr, wbd, bbd, lam, gr, ones_bd,
           batch, seq):
    tm = ROW_TILE
    spb = seq // tm
    row = lambda b, s: (b * spb + s, 0)
    const = lambda b, s: (0, 0)
    n = batch * seq
    return pl.pallas_call(
        _mixer_kernel,
        grid=(batch, spb),
        in_specs=[
            pl.BlockSpec((tm, 2 * D_M), row),
            pl.BlockSpec((tm, D_M), row),
            pl.BlockSpec((tm, D_M), row),
            pl.BlockSpec((tm, D_R), row),
            pl.BlockSpec((tm, D_R), row),
            pl.BlockSpec((2 * NH_M, tm), lambda b, s: (0, b * spb + s)),
            pl.BlockSpec((CONV_W, 2 * D_M), const),
            pl.BlockSpec((1, 2 * D_M), const),
            pl.BlockSpec((1, D_M), const),
            pl.BlockSpec((CONV_W, D_R), const),
            pl.BlockSpec((1, D_R), const),
            pl.BlockSpec((D_R, 2 * D_R), const),
            pl.BlockSpec((1, 2 * D_R), const),
            pl.BlockSpec((1, D_R), const),
            pl.BlockSpec((1, D_R), const),
            pl.BlockSpec((D_R, D_R), const),
        ],
        out_specs=pl.BlockSpec((tm, D_MODEL), row),
        out_shape=jax.ShapeDtypeStruct((n, D_MODEL), BF16),
        scratch_shapes=[
            pltpu.VMEM((SUBLANES, 2 * D_M), F32),
            pltpu.VMEM((SUBLANES, D_R), F32),
            pltpu.VMEM((NH_M, DH_M, 2 * DH_M), F32),
            pltpu.VMEM((NH_M, 1, 1), F32),
            pltpu.VMEM((1, D_R), F32),
        ],
        compiler_params=pltpu.CompilerParams(
            dimension_semantics=("arbitrary", "arbitrary"), vmem_limit_bytes=_vmem_limit(48 << 20)),
        name="seq_mixer",
    )(zqk, v, o, zxr, yr, gt, wcqk, bcqk, gmh, wcr, bcr, wbd, bbd, lam, gr, ones_bd)


def _outproj_kernel(h_ref, hc_ref, w_ref, g_ref, h1_ref, hn_ref):
    h1 = h_ref[...] + jnp.dot(hc_ref[...], w_ref[...], preferred_element_type=F32)
    h1_ref[...] = h1
    hn_ref[...] = _rmsnorm(h1, g_ref[...]).astype(hn_ref.dtype)


def _outproj_norm(h, hcat, w_out, g, hn_dtype):
    n = h.shape[0]
    tm = ROW_TILE
    row = lambda i: (i, 0)
    const = lambda i: (0, 0)
    return pl.pallas_call(
        _outproj_kernel,
        grid=(n // tm,),
        in_specs=[
            pl.BlockSpec((tm, D_MODEL), row),
            pl.BlockSpec((tm, D_MODEL), row),
            pl.BlockSpec((D_MODEL, D_MODEL), const),
            pl.BlockSpec((1, D_MODEL), const),
        ],
        out_specs=[pl.BlockSpec((tm, D_MODEL), row), pl.BlockSpec((tm, D_MODEL), row)],
        out_shape=[jax.ShapeDtypeStruct((n, D_MODEL), F32), jax.ShapeDtypeStruct((n, D_MODEL), hn_dtype)],
        compiler_params=pltpu.CompilerParams(
            dimension_semantics=("arbitrary",), vmem_limit_bytes=_vmem_limit(32 << 20)),
        name="outproj_norm",
    )(h, hcat, w_out, g)


def _router_kernel(hn_ref, wr_ref, ri_ref, rf_ref, cnt_ref, carry_ref):
    tm = hn_ref.shape[0]

    @pl.when(pl.program_id(0) == 0)
    def _():
        carry_ref[...] = jnp.zeros_like(carry_ref)

    lane_i = lax.broadcasted_iota(jnp.int32, (tm, LANES), 1)
    lane = lane_i.astype(F32)
    logits = jnp.dot(hn_ref[...], wr_ref[...], preferred_element_type=F32,
                     precision=lax.Precision.HIGHEST)
    logits = jnp.where(lane_i < N_EXPERTS, logits, -jnp.inf)
    v1 = jnp.max(logits, axis=-1, keepdims=True)
    i1 = jnp.min(jnp.where(logits == v1, lane, float(LANES)), axis=-1, keepdims=True)
    m1 = lane == i1
    rest = jnp.where(m1, -jnp.inf, logits)
    v2 = jnp.max(rest, axis=-1, keepdims=True)
    i2 = jnp.min(jnp.where(rest == v2, lane, float(LANES)), axis=-1, keepdims=True)
    m2 = lane == i2
    e2 = jnp.exp(v2 - v1)
    g1 = 1.0 / (1.0 + e2)
    g2 = e2 / (1.0 + e2)
    onehot = jnp.where(m1 | m2, 1.0, 0.0)
    rr = lax.broadcasted_iota(jnp.int32, (tm, tm), 0)
    cc = lax.broadcasted_iota(jnp.int32, (tm, tm), 1)
    strict = (cc < rr).astype(BF16)
    before = jnp.dot(strict, onehot.astype(BF16), preferred_element_type=F32) + carry_ref[...]
    rank1 = jnp.sum(jnp.where(m1, before, 0.0), axis=-1, keepdims=True).astype(jnp.int32)
    rank2 = jnp.sum(jnp.where(m2, before, 0.0), axis=-1, keepdims=True).astype(jnp.int32)
    carry = carry_ref[...] + jnp.sum(onehot, axis=0, keepdims=True)
    carry_ref[...] = carry
    ri_ref[...] = jnp.where(lane_i == 0, i1.astype(jnp.int32),
                            jnp.where(lane_i == 1, i2.astype(jnp.int32),
                                      jnp.where(lane_i == 2, rank1, rank2)))
    rf_ref[...] = jnp.where(lane_i == 0, g1, g2)
    cnt_ref[...] = jnp.broadcast_to(carry, cnt_ref.shape)


def _router(hn, w_router_pad):
    n = hn.shape[0]
    tm = ROW_TILE
    row = lambda i: (i, 0)
    const = lambda i: (0, 0)
    return pl.pallas_call(
        _router_kernel,
        grid=(n // tm,),
        in_specs=[pl.BlockSpec((tm, D_MODEL), row), pl.BlockSpec((D_MODEL, LANES), const)],
        out_specs=[pl.BlockSpec((tm, LANES), row), pl.BlockSpec((tm, LANES), row),
                   pl.BlockSpec((SUBLANES, LANES), const)],
        out_shape=[jax.ShapeDtypeStruct((n, LANES), jnp.int32), jax.ShapeDtypeStruct((n, LANES), F32),
                   jax.ShapeDtypeStruct((SUBLANES, LANES), F32)],
        scratch_shapes=[pltpu.VMEM((1, LANES), F32)],
        compiler_params=pltpu.CompilerParams(
            dimension_semantics=("arbitrary",), vmem_limit_bytes=_vmem_limit(32 << 20)),
        name="router_top2",
    )(hn, w_router_pad)


ROW_SLABS = D_MODEL // LANES


def _rows_to_2d(ref3):
    return jnp.concatenate([ref3[:, s, :] for s in range(ROW_SLABS)], axis=1)


def _store_rows_3d(ref3, val):
    for s in range(ROW_SLABS):
        ref3[:, s, :] = val[:, s * LANES:(s + 1) * LANES]


def _ffn_kernel(te_ref, ts_ref, nt_ref, x_ref, wg_ref, wu_ref, wd_ref, y_ref, act_ref, *, rows3d):
    i = pl.program_id(0)

    @pl.when(i < nt_ref[0])
    def _():
        x = (_rows_to_2d(x_ref) if rows3d else x_ref[...]).astype(BF16)
        for f0 in range(0, D_FF, FFN_COL_CHUNK):
            g = jnp.dot(x, wg_ref[:, f0:f0 + FFN_COL_CHUNK], preferred_element_type=F32)
            u = jnp.dot(x, wu_ref[:, f0:f0 + FFN_COL_CHUNK], preferred_element_type=F32)
            act_ref[:, f0:f0 + FFN_COL_CHUNK] = (_silu(g) * u).astype(BF16)
        y = jnp.dot(act_ref[...], wd_ref[...], preferred_element_type=F32)
        if rows3d:
            _store_rows_3d(y_ref, y)
        else:
            y_ref[...] = y

    @pl.when(i >= nt_ref[0])
    def _():
        y_ref[...] = jnp.zeros_like(y_ref)


def _grouped_ffn(x, w_gate, w_up, w_down, tile_expert, tile_src, n_tiles, out_rows):
    tm = FFN_ROW_TILE
    t_total = out_rows // tm
    rows3d = x.ndim == 3
    if rows3d:
        x_spec = pl.BlockSpec((tm, ROW_SLABS, LANES), lambda i, te, ts, nt: (ts[i], 0, 0))
        y_spec = pl.BlockSpec((tm, ROW_SLABS, LANES), lambda i, te, ts, nt: (i, 0, 0))
        y_shape = (out_rows, ROW_SLABS, LANES)
    else:
        x_spec = pl.BlockSpec((tm, D_MODEL), lambda i, te, ts, nt: (ts[i], 0))
        y_spec = pl.BlockSpec((tm, D_MODEL), lambda i, te, ts, nt: (i, 0))
        y_shape = (out_rows, D_MODEL)
    grid_spec = pltpu.PrefetchScalarGridSpec(
        num_scalar_prefetch=3,
        grid=(t_total,),
        in_specs=[
            x_spec,
            pl.BlockSpec((None, D_MODEL, D_FF), lambda i, te, ts, nt: (te[i], 0, 0)),
            pl.BlockSpec((None, D_MODEL, D_FF), lambda i, te, ts, nt: (te[i], 0, 0)),
            pl.BlockSpec((None, D_FF, D_MODEL), lambda i, te, ts, nt: (te[i], 0, 0)),
        ],
        out_specs=y_spec,
        scratch_shapes=[pltpu.VMEM((tm, D_FF), BF16)],
    )
    return pl.pallas_call(
        functools.partial(_ffn_kernel, rows3d=rows3d),
        grid_spec=grid_spec,
        out_shape=jax.ShapeDtypeStruct(y_shape, F32),
        compiler_params=pltpu.CompilerParams(
            dimension_semantics=("arbitrary",), vmem_limit_bytes=_vmem_limit(58 << 20)),
        name="grouped_swiglu",
    )(tile_expert, tile_src, n_tiles, x, w_gate, w_up, w_down)


def _dispatch_kernel(dest_ref, zstart_ref, hn_ref, xs_ref, stage_ref, sem):
    tm = hn_ref.shape[0]

    @pl.when(pl.program_id(0) == 0)
    def _():
        stage_ref[...] = jnp.zeros_like(stage_ref)
        per_tile = pl.cdiv(FFN_ROW_TILE, tm)
        fills = [zstart_ref[e] + j * tm for e in range(N_EXPERTS) for j in range(per_tile)]
        last_start = xs_ref.shape[0] - tm
        fills += [jnp.minimum(zstart_ref[N_EXPERTS] + j * tm, last_start)
                  for j in range((N_EXPERTS + 1) * per_tile)]
        for row0 in fills:
            fill = pltpu.make_async_copy(stage_ref, xs_ref.at[pl.ds(row0, tm)], sem)
            fill.start()
            fill.wait()

    _store_rows_3d(stage_ref, hn_ref[...])

    def start(r, carry):
        for k in range(TOP_K):
            d = dest_ref[0, 0, TOP_K * r + k]
            pltpu.make_async_copy(stage_ref.at[r], xs_ref.at[d], sem).start()
        return carry

    lax.fori_loop(0, tm, start, 0)

    def wait(r, carry):
        for k in range(TOP_K):
            pltpu.make_async_copy(stage_ref.at[0], xs_ref.at[0], sem).wait()
        return carry

    lax.fori_loop(0, tm, wait, 0)


def _dispatch(hn, dest_tiles, zero_start, out_rows):
    n = hn.shape[0]
    tm = ROW_TILE
    grid_spec = pltpu.PrefetchScalarGridSpec(
        num_scalar_prefetch=0,
        grid=(n // tm,),
        in_specs=[
            pl.BlockSpec((1, 1, TOP_K * tm), lambda i: (i, 0, 0), memory_space=pltpu.SMEM),
            pl.BlockSpec(memory_space=pltpu.SMEM),
            pl.BlockSpec((tm, D_MODEL), lambda i: (i, 0)),
        ],
        out_specs=pl.BlockSpec(memory_space=pl.ANY),
        scratch_shapes=[pltpu.VMEM((tm, ROW_SLABS, LANES), F32), pltpu.SemaphoreType.DMA(())],
    )
    return pl.pallas_call(
        _dispatch_kernel,
        grid_spec=grid_spec,
        out_shape=jax.ShapeDtypeStruct((out_rows, ROW_SLABS, LANES), F32),
        compiler_params=pltpu.CompilerParams(
            dimension_semantics=("arbitrary",), vmem_limit_bytes=_vmem_limit(32 << 20),
            has_side_effects=True),
        name="expert_dispatch",
    )(dest_tiles, zero_start, hn)


def _ple_tail(h2, p_tile, g_ple, wpg_ref, wpp_ref):
    gate = _sigmoid(jnp.dot(_rmsnorm(h2, g_ple).astype(BF16), wpg_ref[...], preferred_element_type=F32))
    return h2 + gate * jnp.dot(p_tile.astype(BF16), wpp_ref[...], preferred_element_type=F32)


def _combine_kernel(dest_ref, h1_ref, rf_ref, p_ref, gple_ref, wpg_ref, wpp_ref, gfin_ref, ys_ref,
                    out_ref, y0_ref, y1_ref, sem, *, final_norm):
    tm = h1_ref.shape[0]
    bufs = (y0_ref, y1_ref)

    def start(r, carry):
        for k in range(TOP_K):
            d = dest_ref[0, 0, TOP_K * r + k]
            pltpu.make_async_copy(ys_ref.at[d], bufs[k].at[r], sem).start()
        return carry

    lax.fori_loop(0, tm, start, 0)

    def wait(r, carry):
        for k in range(TOP_K):
            pltpu.make_async_copy(ys_ref.at[0], bufs[k].at[0], sem).wait()
        return carry

    lax.fori_loop(0, tm, wait, 0)

    rf = rf_ref[...]
    h2 = h1_ref[...] + rf[:, 0:1] * _rows_to_2d(y0_ref) + rf[:, 1:2] * _rows_to_2d(y1_ref)
    h3 = _ple_tail(h2, p_ref[...], gple_ref[...], wpg_ref, wpp_ref)
    out_ref[...] = _rmsnorm(h3, gfin_ref[...]) if final_norm else h3


def _combine(h1, rf, p, g_ple, w_pg, w_pp, g_final, ys, dest_tiles, final_norm):
    n = h1.shape[0]
    tm = ROW_TILE
    row = lambda i: (i, 0)
    const = lambda i: (0, 0)
    grid_spec = pltpu.PrefetchScalarGridSpec(
        num_scalar_prefetch=0,
        grid=(n // tm,),
        in_specs=[
            pl.BlockSpec((1, 1, TOP_K * tm), lambda i: (i, 0, 0), memory_space=pltpu.SMEM),
            pl.BlockSpec((tm, D_MODEL), row),
            pl.BlockSpec((tm, LANES), row),
            pl.BlockSpec((tm, D_PLE), row),
            pl.BlockSpec((1, D_MODEL), const),
            pl.BlockSpec((D_MODEL, D_MODEL), const),
            pl.BlockSpec((D_PLE, D_MODEL), const),
            pl.BlockSpec((1, D_MODEL), const),
            pl.BlockSpec(memory_space=pl.ANY),
        ],
        out_specs=pl.BlockSpec((tm, D_MODEL), row),
        scratch_shapes=[pltpu.VMEM((tm, ROW_SLABS, LANES), F32), pltpu.VMEM((tm, ROW_SLABS, LANES), F32),
                        pltpu.SemaphoreType.DMA(())],
    )
    return pl.pallas_call(
        functools.partial(_combine_kernel, final_norm=final_norm),
        grid_spec=grid_spec,
        out_shape=jax.ShapeDtypeStruct((n, D_MODEL), F32),
        compiler_params=pltpu.CompilerParams(
            dimension_semantics=("arbitrary",), vmem_limit_bytes=_vmem_limit(40 << 20)),
        name="expert_combine_tail",
    )(dest_tiles, h1, rf, p, g_ple, w_pg, w_pp, g_final, ys)


def _dense_tail_kernel(h1_ref, y_ref, p_ref, gple_ref, wpg_ref, wpp_ref, gfin_ref, out_ref, *,
                       final_norm):
    h2 = h1_ref[...] + y_ref[...]
    h3 = _ple_tail(h2, p_ref[...], gple_ref[...], wpg_ref, wpp_ref)
    out_ref[...] = _rmsnorm(h3, gfin_ref[...]) if final_norm else h3


def _dense_tail(h1, y, p, g_ple, w_pg, w_pp, g_final, final_norm):
    n = h1.shape[0]
    tm = ROW_TILE
    row = lambda i: (i, 0)
    const = lambda i: (0, 0)
    return pl.pallas_call(
        functools.partial(_dense_tail_kernel, final_norm=final_norm),
        grid=(n // tm,),
        in_specs=[
            pl.BlockSpec((tm, D_MODEL), row),
            pl.BlockSpec((tm, D_MODEL), row),
            pl.BlockSpec((tm, D_PLE), row),
            pl.BlockSpec((1, D_MODEL), const),
            pl.BlockSpec((D_MODEL, D_MODEL), const),
            pl.BlockSpec((D_PLE, D_MODEL), const),
            pl.BlockSpec((1, D_MODEL), const),
        ],
        out_specs=pl.BlockSpec((tm, D_MODEL), row),
        out_shape=jax.ShapeDtypeStruct((n, D_MODEL), F32),
        compiler_params=pltpu.CompilerParams(
            dimension_semantics=("arbitrary",), vmem_limit_bytes=_vmem_limit(32 << 20)),
        name="dense_ple_tail",
    )(h1, y, p, g_ple, w_pg, w_pp, g_final)


def _block_diag(w):
    nb, db, _ = w.shape
    eye = jnp.eye(nb, dtype=w.dtype)
    return (eye[:, None, :, None] * w[:, :, None, :]).reshape(nb * db, nb * db)


def _mixer_layer(h, g_mix, w_in, b_in, w_conv_qk, b_conv_qk, g_mh, w_conv_r, b_conv_r,
                 w_ra, b_ra, w_ri, b_ri, lam, g_r, batch, seq):
    q_end, k_end, v_end, o_end = D_M, 2 * D_M, 3 * D_M, 4 * D_M
    i_end = o_end + NH_M
    f_end = i_end + NH_M
    xr_end = f_end + D_R
    w_main = jnp.concatenate([w_in[:, :o_end], w_in[:, f_end:]], axis=1).astype(BF16)
    b_main = jnp.concatenate([b_in[:o_end], b_in[f_end:]])[None, :]
    w_gt = w_in[:, o_end:f_end].T.astype(BF16)
    b_gt = b_in[o_end:f_end][:, None]
    del q_end, k_end, v_end, xr_end
    zqk, v, o, zxr, yr, gt = _norm_inproj(h, g_mix[None, :], w_main, b_main, w_gt, b_gt)
    wbd = jnp.concatenate([_block_diag(w_ra), _block_diag(w_ri)], axis=1).astype(BF16)
    bbd = jnp.concatenate([b_ra, b_ri])[None, :]
    ones_bd = _block_diag(jnp.ones((NB_R, DB_R, DB_R), F32)).astype(BF16)
    return _mixer(zqk, v, o, zxr, yr, gt, w_conv_qk, b_conv_qk[None, :], g_mh[None, :],
                  w_conv_r, b_conv_r[None, :], wbd, bbd, lam[None, :], g_r[None, :], ones_bd,
                  batch, seq)


def kernel(x, p, g_mix, w_in, b_in, w_conv_qk, b_conv_qk, g_mh, w_conv_r, b_conv_r, w_ra, b_ra, w_ri, b_ri, lam, g_r, w_out, g_ffn, w_ff_gate, w_ff_up, w_ff_down, w_router, w_e_gate, w_e_up, w_e_down, g_ple, w_ple_gate, w_ple_proj, g_final):
    batch, seq, _ = x.shape
    n = batch * seq
    depth = g_mix.shape[0]
    h = x.reshape(n, D_MODEL)
    tmf = FFN_ROW_TILE
    for i in range(depth):
        last = i == depth - 1
        hcat = _mixer_layer(h, g_mix[i], w_in[i], b_in[i], w_conv_qk[i], b_conv_qk[i], g_mh[i],
                            w_conv_r[i], b_conv_r[i], w_ra[i], b_ra[i], w_ri[i], b_ri[i], lam[i],
                            g_r[i], batch, seq)
        p_i = p[i].reshape(n, D_PLE)
        w_pg = w_ple_gate[i].astype(BF16)
        w_pp = w_ple_proj[i].astype(BF16)
        j = i // 2
        if i % 2 == 0:
            h1, hn = _outproj_norm(h, hcat, w_out[i].astype(BF16), g_ffn[i][None, :], BF16)
            t_total = n // tmf
            tiles = jnp.arange(t_total, dtype=jnp.int32)
            y = _grouped_ffn(hn, w_ff_gate[j][None].astype(BF16), w_ff_up[j][None].astype(BF16),
                             w_ff_down[j][None].astype(BF16), jnp.zeros((t_total,), jnp.int32), tiles,
                             jnp.full((1,), t_total, jnp.int32), n)
            h = _dense_tail(h1, y, p_i, g_ple[i][None, :], w_pg, w_pp, g_final[None, :], last)
        else:
            h1, hn = _outproj_norm(h, hcat, w_out[i].astype(BF16), g_ffn[i][None, :], F32)
            w_r = jnp.zeros((D_MODEL, LANES), F32).at[:, :N_EXPERTS].set(w_router[j])
            ri, rf, cnt = _router(hn, w_r)
            counts = cnt[0, :N_EXPERTS].astype(jnp.int32)
            padded = (counts + tmf - 1) // tmf * tmf
            p_end = jnp.cumsum(padded)
            p_start = p_end - padded
            t_total = (n * TOP_K) // tmf + N_EXPERTS
            rows_total = t_total * tmf + pl.cdiv(tmf, ROW_TILE) * ROW_TILE
            n_tiles = (p_end[-1] // tmf).astype(jnp.int32)
            tiles = jnp.arange(t_total, dtype=jnp.int32)
            tile_expert = jnp.minimum(
                jnp.searchsorted(p_end, tiles * tmf, side='right'), N_EXPERTS - 1).astype(jnp.int32)
            tile_src = jnp.minimum(tiles, n_tiles - 1)
            dest = (p_start[ri[:, :TOP_K]] + ri[:, TOP_K:2 * TOP_K]).astype(jnp.int32)
            dest_tiles = dest.reshape(n // ROW_TILE, 1, TOP_K * ROW_TILE)
            zero_start = jnp.concatenate([p_start + counts, p_end[-1:]]).astype(jnp.int32)
            xs = _dispatch(hn, dest_tiles, zero_start, rows_total)
            ys = _grouped_ffn(xs, w_e_gate[j].astype(BF16), w_e_up[j].astype(BF16),
                              w_e_down[j].astype(BF16), tile_expert, tile_src,
                              n_tiles.reshape(1), t_total * tmf)
            h = _combine(h1, rf, p_i, g_ple[i][None, :], w_pg, w_pp, g_final[None, :], ys, dest_tiles,
                         last)
    return h.reshape(batch, seq, D_MODEL)
```

```python
import functools

import jax
import jax.numpy as jnp
from jax import lax
from jax.experimental import pallas as pl
from jax.experimental.pallas import tpu as pltpu

F32 = jnp.float32
BF16 = jnp.bfloat16

D_MODEL = 1024
NH_M = 4
DH_M = 128
D_M = NH_M * DH_M
D_R = D_MODEL - D_M
NB_R = 8
DB_R = D_R // NB_R
CONV_W = 4
LRU_C = 8.0
D_FF = 2816
N_EXPERTS = 8
TOP_K = 2
D_PLE = 256
EPS = 1e-6

LANES = 128
SUBLANES = 8
V7X_VMEM_BYTES = 64 * 1024 * 1024

ROW_TILE = 512
MIX_CHUNK = 128
FFN_ROW_TILE = 512
FFN_COL_CHUNK = 256


def _vmem_limit(nbytes):
    return int(min(nbytes, V7X_VMEM_BYTES - 6 * 1024 * 1024))


def _rmsnorm(x, g):
    return x * lax.rsqrt(jnp.mean(x * x, axis=-1, keepdims=True) + EPS) * g


def _sigmoid(x):
    return 1.0 / (1.0 + jnp.exp(-x))


def _silu(x):
    return x * _sigmoid(x)


def _inproj_kernel(x_ref, g_ref, w_ref, b_ref, wg_ref, bg_ref,
                   zqk_ref, v_ref, o_ref, zxr_ref, yr_ref, gt_ref):
    xn = _rmsnorm(x_ref[...], g_ref[...]).astype(BF16)

    def seg(lo, hi):
        return jnp.dot(xn, w_ref[:, lo:hi], preferred_element_type=F32) + b_ref[:, lo:hi]

    zqk_ref[...] = seg(0, 2 * D_M)
    v_ref[...] = seg(2 * D_M, 3 * D_M).astype(BF16)
    o_ref[...] = seg(3 * D_M, 4 * D_M)
    zxr_ref[...] = seg(4 * D_M, 4 * D_M + D_R)
    yr_ref[...] = seg(4 * D_M + D_R, 4 * D_M + 2 * D_R)
    gt_ref[...] = lax.dot_general(wg_ref[...], xn, (((1,), (1,)), ((), ())),
                                  preferred_element_type=F32) + bg_ref[...]


def _norm_inproj(h, g, w_main, b_main, w_gt, b_gt):
    n = h.shape[0]
    tm = ROW_TILE
    wcols = w_main.shape[1]
    row = lambda i: (i, 0)
    const = lambda i: (0, 0)
    return pl.pallas_call(
        _inproj_kernel,
        grid=(n // tm,),
        in_specs=[
            pl.BlockSpec((tm, D_MODEL), row),
            pl.BlockSpec((1, D_MODEL), const),
            pl.BlockSpec((D_MODEL, wcols), const),
            pl.BlockSpec((1, wcols), const),
            pl.BlockSpec((2 * NH_M, D_MODEL), const),
            pl.BlockSpec((2 * NH_M, 1), const),
        ],
        out_specs=[
            pl.BlockSpec((tm, 2 * D_M), row),
            pl.BlockSpec((tm, D_M), row),
            pl.BlockSpec((tm, D_M), row),
            pl.BlockSpec((tm, D_R), row),
            pl.BlockSpec((tm, D_R), row),
            pl.BlockSpec((2 * NH_M, tm), lambda i: (0, i)),
        ],
        out_shape=[
            jax.ShapeDtypeStruct((n, 2 * D_M), F32),
            jax.ShapeDtypeStruct((n, D_M), BF16),
            jax.ShapeDtypeStruct((n, D_M), F32),
            jax.ShapeDtypeStruct((n, D_R), F32),
            jax.ShapeDtypeStruct((n, D_R), F32),
            jax.ShapeDtypeStruct((2 * NH_M, n), F32),
        ],
        compiler_params=pltpu.CompilerParams(
            dimension_semantics=("arbitrary",), vmem_limit_bytes=_vmem_limit(48 << 20)),
        name="norm_inproj",
    )(h, g, w_main, b_main, w_gt, b_gt)


def _causal_conv(x, prev, w, b):
    t = x.shape[0]
    xe = jnp.concatenate([prev, x], axis=0)
    acc = xe[SUBLANES:] * w[CONV_W - 1:CONV_W, :] + b
    for j in range(CONV_W - 1):
        shifted = pltpu.roll(xe, CONV_W - 1 - j, axis=0)[SUBLANES:]
        acc = acc + shifted * w[j:j + 1, :]
    del t
    return acc


def _mixer_kernel(zqk_ref, v_ref, o_ref, zxr_ref, yr_ref, gt_ref,
                  wcqk_ref, bcqk_ref, gmh_ref, wcr_ref, bcr_ref, wbd_ref, bbd_ref,
                  lam_ref, gr_ref, ones_ref,
                  out_ref,
                  pqk_ref, pxr_ref, s_ref, m_ref, hr_ref):
    tm = zqk_ref.shape[0]
    first = pl.program_id(1) == 0

    @pl.when(first)
    def _():
        pqk_ref[...] = jnp.zeros_like(pqk_ref)
        pxr_ref[...] = jnp.zeros_like(pxr_ref)
        s_ref[...] = jnp.zeros_like(s_ref)
        m_ref[...] = jnp.zeros_like(m_ref)
        hr_ref[...] = jnp.zeros_like(hr_ref)

    zqk = zqk_ref[...]
    qk = _silu(_causal_conv(zqk, pqk_ref[...], wcqk_ref[...], bcqk_ref[...]))
    pqk_ref[...] = zqk[tm - SUBLANES:, :]

    gt = gt_ref[...]
    logf = jnp.minimum(gt, 0.0) - jnp.log(1.0 + jnp.exp(-jnp.abs(gt)))
    L = MIX_CHUNK
    ri = lax.broadcasted_iota(jnp.int32, (L, L), 0)
    ci = lax.broadcasted_iota(jnp.int32, (L, L), 1)
    tril = ci <= ri
    upper = (ri <= ci).astype(F32)
    ones_blk = jnp.ones((L, DH_M), BF16)
    scale = DH_M ** -0.5
    neg_inf = -jnp.inf

    b_all = jnp.concatenate(
        [jnp.dot(logf[:, c * L:(c + 1) * L], upper, preferred_element_type=F32,
                 precision=lax.Precision.HIGHEST) for c in range(tm // L)], axis=1)

    for h in range(NH_M):
        s_state = s_ref[h]
        m_prev = m_ref[h]
        for c in range(tm // L):
            r0 = c * L
            q = (qk[r0:r0 + L, h * DH_M:(h + 1) * DH_M] * scale).astype(BF16)
            k32 = qk[r0:r0 + L, D_M + h * DH_M:D_M + (h + 1) * DH_M]
            kt32 = k32.T
            v_aug = jnp.concatenate([v_ref[r0:r0 + L, h * DH_M:(h + 1) * DH_M], ones_blk], axis=1)
            ig_row = gt[h:h + 1, r0:r0 + L]
            lf_row = logf[NH_M + h:NH_M + h + 1, r0:r0 + L]
            b_row = b_all[NH_M + h:NH_M + h + 1, r0:r0 + L]
            c_row = ig_row - b_row
            g_col = jnp.maximum(jnp.max(jnp.where(tril, c_row, neg_inf), axis=-1, keepdims=True), m_prev)
            b_col = jnp.sum(jnp.where(tril, lf_row, 0.0), axis=-1, keepdims=True)
            g_last = g_col[L - 1:L, :]
            b_last = b_col[L - 1:L, :]
            dmat = jnp.exp(jnp.where(tril, c_row - g_col, neg_inf))
            s = jnp.dot(q, kt32.astype(BF16), preferred_element_type=F32) * dmat
            inter = jnp.exp(m_prev - g_col)
            out = (jnp.dot(s.astype(BF16), v_aug, preferred_element_type=F32)
                   + inter * jnp.dot(q, s_state.astype(BF16), preferred_element_type=F32))
            num = out[:, :DH_M]
            den = out[:, DH_M:]
            hm = num / jnp.maximum(jnp.abs(den), jnp.exp(-(b_col + g_col)))
            hm = hm * _sigmoid(o_ref[r0:r0 + L, h * DH_M:(h + 1) * DH_M])
            hm = hm * lax.rsqrt(jnp.mean(hm * hm, axis=-1, keepdims=True) + EPS)
            out_ref[r0:r0 + L, h * DH_M:(h + 1) * DH_M] = (
                hm * gmh_ref[:, h * DH_M:(h + 1) * DH_M]).astype(out_ref.dtype)
            wk_row = jnp.exp(c_row - g_last)
            decay = jnp.exp(m_prev - g_last)
            s_state = decay * s_state + jnp.dot((kt32 * wk_row).astype(BF16), v_aug,
                                                preferred_element_type=F32)
            m_prev = b_last + g_last
        s_ref[h] = s_state
        m_ref[h] = m_prev

    zxr = zxr_ref[...]
    xr = _causal_conv(zxr, pxr_ref[...], wcr_ref[...], bcr_ref[...])
    pxr_ref[...] = zxr[tm - SUBLANES:, :]
    gates = _sigmoid(jnp.dot(xr.astype(BF16), wbd_ref[...], preferred_element_type=F32) + bbd_ref[...])
    r_gate = gates[:, :D_R]
    i_gate = gates[:, D_R:]
    nlam = -lam_ref[...]
    softplus = jnp.maximum(nlam, 0.0) + jnp.log(1.0 + jnp.exp(-jnp.abs(nlam)))
    log_a = (-LRU_C) * r_gate * softplus
    a = jnp.exp(log_a)
    u = jnp.sqrt(1.0 - jnp.exp(2.0 * log_a)) * (i_gate * xr)
    row8 = lax.broadcasted_iota(jnp.int32, (tm, D_R), 0) & (SUBLANES - 1)
    for sft in (1, 2, 4):
        keep = row8 >= sft
        a_sh = pltpu.roll(a, sft, axis=0)
        u_sh = pltpu.roll(u, sft, axis=0)
        u = jnp.where(keep, a * u_sh + u, u)
        a = jnp.where(keep, a * a_sh, a)
    hprev = hr_ref[...]
    rows = []
    for gi in range(tm // SUBLANES):
        blk = u[gi * SUBLANES:(gi + 1) * SUBLANES] + a[gi * SUBLANES:(gi + 1) * SUBLANES] * hprev
        rows.append(blk)
        hprev = blk[SUBLANES - 1:SUBLANES, :]
    hr_ref[...] = hprev
    hseq = jnp.concatenate(rows, axis=0)
    yr = yr_ref[...]
    gelu = 0.5 * yr * (1.0 + jnp.tanh(0.7978845608028654 * (yr + 0.044715 * yr * yr * yr)))
    hr = hseq * gelu
    sq = hr * hr
    sq_hi = sq.astype(BF16)
    sq_lo = (sq - sq_hi.astype(F32)).astype(BF16)
    gsum = (jnp.dot(sq_hi, ones_ref[...], preferred_element_type=F32)
            + jnp.dot(sq_lo, ones_ref[...], preferred_element_type=F32))
    hr = hr * lax.rsqrt(gsum * (1.0 / DB_R) + EPS) * gr_ref[...]
    out_ref[:, D_M:] = hr.astype(out_ref.dtype)


def _mixer(zqk, v, o, zxr, yr, gt, wcqk, bcqk, gmh, wcr, bcr, wbd, bbd, lam, gr, ones_bd,
           batch, seq):
    tm = ROW_TILE
    spb = seq // tm
    row = lambda b, s: (b * spb + s, 0)
    const = lambda b, s: (0, 0)
    n = batch * seq
    return pl.pallas_call(
        _mixer_kernel,
        grid=(batch, spb),
        in_specs=[
            pl.BlockSpec((tm, 2 * D_M), row),
            pl.BlockSpec((tm, D_M), row),
            pl.BlockSpec((tm, D_M), row),
            pl.BlockSpec((tm, D_R), row),
            pl.BlockSpec((tm, D_R), row),
            pl.BlockSpec((2 * NH_M, tm), lambda b, s: (0, b * spb + s)),
            pl.BlockSpec((CONV_W, 2 * D_M), const),
            pl.BlockSpec((1, 2 * D_M), const),
            pl.BlockSpec((1, D_M), const),
            pl.BlockSpec((CONV_W, D_R), const),
            pl.BlockSpec((1, D_R), const),
            pl.BlockSpec((D_R, 2 * D_R), const),
            pl.BlockSpec((1, 2 * D_R), const),
            pl.BlockSpec((1, D_R), const),
            pl.BlockSpec((1, D_R), const),
            pl.BlockSpec((D_R, D_R), const),
        ],
        out_specs=pl.BlockSpec((tm, D_MODEL), row),
        out_shape=jax.ShapeDtypeStruct((n, D_MODEL), BF16),
        scratch_shapes=[
            pltpu.VMEM((SUBLANES, 2 * D_M), F32),
            pltpu.VMEM((SUBLANES, D_R), F32),
            pltpu.VMEM((NH_M, DH_M, 2 * DH_M), F32),
            pltpu.VMEM((NH_M, 1, 1), F32),
            pltpu.VMEM((1, D_R), F32),
        ],
        compiler_params=pltpu.CompilerParams(
            dimension_semantics=("arbitrary", "arbitrary"), vmem_limit_bytes=_vmem_limit(48 << 20)),
        name="seq_mixer",
    )(zqk, v, o, zxr, yr, gt, wcqk, bcqk, gmh, wcr, bcr, wbd, bbd, lam, gr, ones_bd)


def _outproj_kernel(h_ref, hc_ref, w_ref, g_ref, h1_ref, hn_ref):
    h1 = h_ref[...] + jnp.dot(hc_ref[...], w_ref[...], preferred_element_type=F32)
    h1_ref[...] = h1
    hn_ref[...] = _rmsnorm(h1, g_ref[...]).astype(hn_ref.dtype)


def _outproj_norm(h, hcat, w_out, g, hn_dtype):
    n = h.shape[0]
    tm = ROW_TILE
    row = lambda i: (i, 0)
    const = lambda i: (0, 0)
    return pl.pallas_call(
        _outproj_kernel,
        grid=(n // tm,),
        in_specs=[
            pl.BlockSpec((tm, D_MODEL), row),
            pl.BlockSpec((tm, D_MODEL), row),
            pl.BlockSpec((D_MODEL, D_MODEL), const),
            pl.BlockSpec((1, D_MODEL), const),
        ],
        out_specs=[pl.BlockSpec((tm, D_MODEL), row), pl.BlockSpec((tm, D_MODEL), row)],
        out_shape=[jax.ShapeDtypeStruct((n, D_MODEL), F32), jax.ShapeDtypeStruct((n, D_MODEL), hn_dtype)],
        compiler_params=pltpu.CompilerParams(
            dimension_semantics=("arbitrary",), vmem_limit_bytes=_vmem_limit(32 << 20)),
        name="outproj_norm",
    )(h, hcat, w_out, g)


def _outproj_router_kernel(h_ref, hc_ref, w_ref, g_ref, wr_ref, h1_ref, hn_ref, ri_ref, rf_ref, cnt_ref,
                           carry_ref):
    tm = h_ref.shape[0]

    @pl.when(pl.program_id(0) == 0)
    def _():
        carry_ref[...] = jnp.zeros_like(carry_ref)

    h1 = h_ref[...] + jnp.dot(hc_ref[...], w_ref[...], preferred_element_type=F32)
    h1_ref[...] = h1
    hn = _rmsnorm(h1, g_ref[...])
    hn_ref[...] = hn
    lane_i = lax.broadcasted_iota(jnp.int32, (tm, LANES), 1)
    lane = lane_i.astype(F32)
    logits = jnp.dot(hn, wr_ref[...], preferred_element_type=F32, precision=lax.Precision.HIGHEST)
    logits = jnp.where(lane_i < N_EXPERTS, logits, -jnp.inf)
    v1 = jnp.max(logits, axis=-1, keepdims=True)
    i1 = jnp.min(jnp.where(logits == v1, lane, float(LANES)), axis=-1, keepdims=True)
    m1 = lane == i1
    rest = jnp.where(m1, -jnp.inf, logits)
    v2 = jnp.max(rest, axis=-1, keepdims=True)
    i2 = jnp.min(jnp.where(rest == v2, lane, float(LANES)), axis=-1, keepdims=True)
    m2 = lane == i2
    e2 = jnp.exp(v2 - v1)
    g1 = 1.0 / (1.0 + e2)
    g2 = e2 / (1.0 + e2)
    carry = carry_ref[...] + jnp.sum(jnp.where(m1 | m2, 1.0, 0.0), axis=0, keepdims=True)
    carry_ref[...] = carry
    ri_ref[...] = jnp.where(lane_i == 0, i1.astype(jnp.int32), i2.astype(jnp.int32))
    rf_ref[...] = jnp.where(lane_i == 0, g1, g2)
    cnt_ref[...] = jnp.broadcast_to(carry, cnt_ref.shape)


def _outproj_router(h, hcat, w_out, g, w_router_pad):
    n = h.shape[0]
    tm = ROW_TILE
    row = lambda i: (i, 0)
    const = lambda i: (0, 0)
    return pl.pallas_call(
        _outproj_router_kernel,
        grid=(n // tm,),
        in_specs=[
            pl.BlockSpec((tm, D_MODEL), row),
            pl.BlockSpec((tm, D_MODEL), row),
            pl.BlockSpec((D_MODEL, D_MODEL), const),
            pl.BlockSpec((1, D_MODEL), const),
            pl.BlockSpec((D_MODEL, LANES), const),
        ],
        out_specs=[pl.BlockSpec((tm, D_MODEL), row), pl.BlockSpec((tm, D_MODEL), row),
                   pl.BlockSpec((tm, LANES), row), pl.BlockSpec((tm, LANES), row),
                   pl.BlockSpec((SUBLANES, LANES), const)],
        out_shape=[jax.ShapeDtypeStruct((n, D_MODEL), F32), jax.ShapeDtypeStruct((n, D_MODEL), F32),
                   jax.ShapeDtypeStruct((n, LANES), jnp.int32), jax.ShapeDtypeStruct((n, LANES), F32),
                   jax.ShapeDtypeStruct((SUBLANES, LANES), F32)],
        scratch_shapes=[pltpu.VMEM((1, LANES), F32)],
        compiler_params=pltpu.CompilerParams(
            dimension_semantics=("arbitrary",), vmem_limit_bytes=_vmem_limit(40 << 20)),
        name="outproj_router",
    )(h, hcat, w_out, g, w_router_pad)


def _swiglu(x, wg_ref, wu_ref, wd_ref, act_ref):
    for f0 in range(0, D_FF, FFN_COL_CHUNK):
        g = jnp.dot(x, wg_ref[:, f0:f0 + FFN_COL_CHUNK], preferred_element_type=F32)
        u = jnp.dot(x, wu_ref[:, f0:f0 + FFN_COL_CHUNK], preferred_element_type=F32)
        act_ref[:, f0:f0 + FFN_COL_CHUNK] = (_silu(g) * u).astype(BF16)
    return jnp.dot(act_ref[...], wd_ref[...], preferred_element_type=F32)


def _dense_ffn_kernel(x_ref, wg_ref, wu_ref, wd_ref, y_ref, act_ref):
    y_ref[...] = _swiglu(x_ref[...], wg_ref, wu_ref, wd_ref, act_ref)


def _dense_ffn(x, w_gate, w_up, w_down):
    n = x.shape[0]
    tm = FFN_ROW_TILE
    row = lambda i: (i, 0)
    const = lambda i: (0, 0)
    return pl.pallas_call(
        _dense_ffn_kernel,
        grid=(n // tm,),
        in_specs=[
            pl.BlockSpec((tm, D_MODEL), row),
            pl.BlockSpec((D_MODEL, D_FF), const),
            pl.BlockSpec((D_MODEL, D_FF), const),
            pl.BlockSpec((D_FF, D_MODEL), const),
        ],
        out_specs=pl.BlockSpec((tm, D_MODEL), row),
        out_shape=jax.ShapeDtypeStruct((n, D_MODEL), F32),
        scratch_shapes=[pltpu.VMEM((tm, D_FF), BF16)],
        compiler_params=pltpu.CompilerParams(
            dimension_semantics=("arbitrary",), vmem_limit_bytes=_vmem_limit(58 << 20)),
        name="dense_swiglu",
    )(x, w_gate, w_up, w_down)


KEY_STRIDE = 1 << 16


def _moe_ffn_kernel(te_ref, nt_ref, keys_ref, keys_next_ref, hn_ref, wg_ref, wu_ref, wd_ref, out_ref,
                    xbuf, ybuf, act_ref, sem_g, sem_s, *, n_tok):
    i = pl.program_id(0)
    nt = nt_ref[0]
    tmf = xbuf.shape[1]
    slot = i % 2
    dump0 = TOP_K * n_tok

    def row_ids(kref, r):
        a = kref[0, 0, r] & (KEY_STRIDE - 1)
        return a, a < TOP_K * n_tok

    def gather(kref, slot_):
        def body(j, carry):
            for u in range(2):
                r = 2 * j + u
                a, valid = row_ids(kref, r)
                tok = jnp.where(valid, a >> 1, 0)
                pltpu.make_async_copy(hn_ref.at[pl.ds(tok, 1)], xbuf.at[slot_, pl.ds(r, 1)],
                                      sem_g.at[slot_]).start(priority=u)
            return carry
        lax.fori_loop(0, tmf // 2, body, 0)

    def wait_rows(buf, sem):
        pltpu.make_async_copy(hn_ref.at[pl.ds(0, tmf)], buf, sem).wait()

    @pl.when(i == 0)
    def _():
        ybuf[...] = jnp.zeros_like(ybuf)
        for s in range(2):
            fill = pltpu.make_async_copy(ybuf.at[s], out_ref.at[pl.ds(dump0 + s * tmf, tmf)], sem_s.at[s])
            fill.start()
            fill.wait()
        gather(keys_ref, 0)

    @pl.when(i + 1 < nt)
    def _():
        gather(keys_next_ref, 1 - slot)

    @pl.when(i < nt)
    def _():
        wait_rows(xbuf.at[slot], sem_g.at[slot])
        y = _swiglu(xbuf[slot].astype(BF16), wg_ref, wu_ref, wd_ref, act_ref)

        @pl.when(i >= 2)
        def _():
            wait_rows(ybuf.at[slot], sem_s.at[slot])

        ybuf[slot] = y

        def body(j, carry):
            for u in range(2):
                r = 2 * j + u
                a, valid = row_ids(keys_ref, r)
                dst = jnp.where(valid, (a & 1) * n_tok + (a >> 1), dump0 + slot * tmf + r)
                pltpu.make_async_copy(ybuf.at[slot, pl.ds(r, 1)], out_ref.at[pl.ds(dst, 1)],
                                      sem_s.at[slot]).start(priority=u)
            return carry
        lax.fori_loop(0, tmf // 2, body, 0)

    @pl.when(i == pl.num_programs(0) - 1)
    def _():
        @pl.when(nt >= 2)
        def _():
            wait_rows(ybuf.at[nt % 2], sem_s.at[nt % 2])
        wait_rows(ybuf.at[(nt - 1) % 2], sem_s.at[(nt - 1) % 2])


def _moe_ffn(hn, w_gate, w_up, w_down, keys_tiles, tile_expert, n_tiles):
    n_tok = hn.shape[0]
    tmf = FFN_ROW_TILE
    t_total = keys_tiles.shape[0]
    grid_spec = pltpu.PrefetchScalarGridSpec(
        num_scalar_prefetch=2,
        grid=(t_total,),
        in_specs=[
            pl.BlockSpec((1, 1, tmf), lambda i, te, nt: (i, 0, 0), memory_space=pltpu.SMEM),
            pl.BlockSpec((1, 1, tmf), lambda i, te, nt: (jnp.minimum(i + 1, t_total - 1), 0, 0),
                         memory_space=pltpu.SMEM),
            pl.BlockSpec(memory_space=pl.ANY),
            pl.BlockSpec((None, D_MODEL, D_FF), lambda i, te, nt: (te[i], 0, 0)),
            pl.BlockSpec((None, D_MODEL, D_FF), lambda i, te, nt: (te[i], 0, 0)),
            pl.BlockSpec((None, D_FF, D_MODEL), lambda i, te, nt: (te[i], 0, 0)),
        ],
        out_specs=pl.BlockSpec(memory_space=pl.ANY),
        scratch_shapes=[
            pltpu.VMEM((2, tmf, D_MODEL), F32),
            pltpu.VMEM((2, tmf, D_MODEL), F32),
            pltpu.VMEM((tmf, D_FF), BF16),
            pltpu.SemaphoreType.DMA((2,)),
            pltpu.SemaphoreType.DMA((2,)),
        ],
    )
    return pl.pallas_call(
        functools.partial(_moe_ffn_kernel, n_tok=n_tok),
        grid_spec=grid_spec,
        out_shape=jax.ShapeDtypeStruct((TOP_K * n_tok + 2 * tmf, D_MODEL), F32),
        compiler_params=pltpu.CompilerParams(
            dimension_semantics=("arbitrary",), vmem_limit_bytes=_vmem_limit(58 << 20),
            has_side_effects=True),
        name="expert_swiglu",
    )(tile_expert, n_tiles, keys_tiles, keys_tiles, hn, w_gate, w_up, w_down)


def _ple_tail(h2, p_tile, g_ple, wpg_ref, wpp_ref):
    gate = _sigmoid(jnp.dot(_rmsnorm(h2, g_ple).astype(BF16), wpg_ref[...], preferred_element_type=F32))
    return h2 + gate * jnp.dot(p_tile.astype(BF16), wpp_ref[...], preferred_element_type=F32)


def _combine_kernel(h1_ref, rf_ref, y0_ref, y1_ref, p_ref, gple_ref, wpg_ref, wpp_ref, gfin_ref,
                    out_ref, *, final_norm):
    rf = rf_ref[...]
    h2 = h1_ref[...] + rf[:, 0:1] * y0_ref[...] + rf[:, 1:2] * y1_ref[...]
    h3 = _ple_tail(h2, p_ref[...], gple_ref[...], wpg_ref, wpp_ref)
    out_ref[...] = _rmsnorm(h3, gfin_ref[...]) if final_norm else h3


def _combine(h1, rf, ys, p, g_ple, w_pg, w_pp, g_final, final_norm):
    n = h1.shape[0]
    tm = ROW_TILE
    row = lambda i: (i, 0)
    const = lambda i: (0, 0)
    return pl.pallas_call(
        functools.partial(_combine_kernel, final_norm=final_norm),
        grid=(n // tm,),
        in_specs=[
            pl.BlockSpec((tm, D_MODEL), row),
            pl.BlockSpec((tm, LANES), row),
            pl.BlockSpec((tm, D_MODEL), row),
            pl.BlockSpec((tm, D_MODEL), lambda i: (n // tm + i, 0)),
            pl.BlockSpec((tm, D_PLE), row),
            pl.BlockSpec((1, D_MODEL), const),
            pl.BlockSpec((D_MODEL, D_MODEL), const),
            pl.BlockSpec((D_PLE, D_MODEL), const),
            pl.BlockSpec((1, D_MODEL), const),
        ],
        out_specs=pl.BlockSpec((tm, D_MODEL), row),
        out_shape=jax.ShapeDtypeStruct((n, D_MODEL), F32),
        compiler_params=pltpu.CompilerParams(
            dimension_semantics=("arbitrary",), vmem_limit_bytes=_vmem_limit(40 << 20)),
        name="expert_combine_tail",
    )(h1, rf, ys, ys, p, g_ple, w_pg, w_pp, g_final)


def _dense_tail_kernel(h1_ref, y_ref, p_ref, gple_ref, wpg_ref, wpp_ref, gfin_ref, out_ref, *,
                       final_norm):
    h2 = h1_ref[...] + y_ref[...]
    h3 = _ple_tail(h2, p_ref[...], gple_ref[...], wpg_ref, wpp_ref)
    out_ref[...] = _rmsnorm(h3, gfin_ref[...]) if final_norm else h3


def _dense_tail(h1, y, p, g_ple, w_pg, w_pp, g_final, final_norm):
    n = h1.shape[0]
    tm = ROW_TILE
    row = lambda i: (i, 0)
    const = lambda i: (0, 0)
    return pl.pallas_call(
        functools.partial(_dense_tail_kernel, final_norm=final_norm),
        grid=(n // tm,),
        in_specs=[
            pl.BlockSpec((tm, D_MODEL), row),
            pl.BlockSpec((tm, D_MODEL), row),
            pl.BlockSpec((tm, D_PLE), row),
            pl.BlockSpec((1, D_MODEL), const),
            pl.BlockSpec((D_MODEL, D_MODEL), const),
            pl.BlockSpec((D_PLE, D_MODEL), const),
            pl.BlockSpec((1, D_MODEL), const),
        ],
        out_specs=pl.BlockSpec((tm, D_MODEL), row),
        out_shape=jax.ShapeDtypeStruct((n, D_MODEL), F32),
        compiler_params=pltpu.CompilerParams(
            dimension_semantics=("arbitrary",), vmem_limit_bytes=_vmem_limit(32 << 20)),
        name="dense_ple_tail",
    )(h1, y, p, g_ple, w_pg, w_pp, g_final)


def _block_diag(w):
    nb, db, _ = w.shape
    eye = jnp.eye(nb, dtype=w.dtype)
    return (eye[:, None, :, None] * w[:, :, None, :]).reshape(nb * db, nb * db)


def _mixer_layer(h, g_mix, w_in, b_in, w_conv_qk, b_conv_qk, g_mh, w_conv_r, b_conv_r,
                 w_ra, b_ra, w_ri, b_ri, lam, g_r, batch, seq):
    q_end, k_end, v_end, o_end = D_M, 2 * D_M, 3 * D_M, 4 * D_M
    i_end = o_end + NH_M
    f_end = i_end + NH_M
    xr_end = f_end + D_R
    w_main = jnp.concatenate([w_in[:, :o_end], w_in[:, f_end:]], axis=1).astype(BF16)
    b_main = jnp.concatenate([b_in[:o_end], b_in[f_end:]])[None, :]
    w_gt = w_in[:, o_end:f_end].T.astype(BF16)
    b_gt = b_in[o_end:f_end][:, None]
    del q_end, k_end, v_end, xr_end
    zqk, v, o, zxr, yr, gt = _norm_inproj(h, g_mix[None, :], w_main, b_main, w_gt, b_gt)
    wbd = jnp.concatenate([_block_diag(w_ra), _block_diag(w_ri)], axis=1).astype(BF16)
    bbd = jnp.concatenate([b_ra, b_ri])[None, :]
    ones_bd = _block_diag(jnp.ones((NB_R, DB_R, DB_R), F32)).astype(BF16)
    return _mixer(zqk, v, o, zxr, yr, gt, w_conv_qk, b_conv_qk[None, :], g_mh[None, :],
                  w_conv_r, b_conv_r[None, :], wbd, bbd, lam[None, :], g_r[None, :], ones_bd,
                  batch, seq)


def _expert_tiles(top_idx, counts_f, n_tok):
    tmf = FFN_ROW_TILE
    counts = counts_f.astype(jnp.int32)
    padded = (counts + tmf - 1) // tmf * tmf
    n_tiles = (jnp.sum(padded) // tmf).astype(jnp.int32).reshape(1)
    a_ids = jnp.arange(TOP_K * n_tok, dtype=jnp.int32)
    real = top_idx.reshape(-1) * KEY_STRIDE + a_ids
    pad_j = jnp.arange(tmf, dtype=jnp.int32)[None, :]
    experts = jnp.arange(N_EXPERTS, dtype=jnp.int32)[:, None]
    pad_keys = jnp.where(pad_j < (padded - counts)[:, None],
                         experts * KEY_STRIDE + TOP_K * n_tok + pad_j, N_EXPERTS * KEY_STRIDE)
    keys = jnp.sort(jnp.concatenate([real, pad_keys.reshape(-1)]))
    t_total = (TOP_K * n_tok) // tmf + N_EXPERTS
    keys_tiles = keys.reshape(t_total, 1, tmf)
    tile_expert = jnp.minimum(keys_tiles[:, 0, 0] // KEY_STRIDE, N_EXPERTS - 1).astype(jnp.int32)
    return keys_tiles, tile_expert, n_tiles


def kernel(x, p, g_mix, w_in, b_in, w_conv_qk, b_conv_qk, g_mh, w_conv_r, b_conv_r, w_ra, b_ra, w_ri, b_ri, lam, g_r, w_out, g_ffn, w_ff_gate, w_ff_up, w_ff_down, w_router, w_e_gate, w_e_up, w_e_down, g_ple, w_ple_gate, w_ple_proj, g_final):
    batch, seq, _ = x.shape
    n = batch * seq
    depth = g_mix.shape[0]
    h = x.reshape(n, D_MODEL)
    tmf = FFN_ROW_TILE
    for i in range(depth):
        last = i == depth - 1
        hcat = _mixer_layer(h, g_mix[i], w_in[i], b_in[i], w_conv_qk[i], b_conv_qk[i], g_mh[i],
                            w_conv_r[i], b_conv_r[i], w_ra[i], b_ra[i], w_ri[i], b_ri[i], lam[i],
                            g_r[i], batch, seq)
        p_i = p[i].reshape(n, D_PLE)
        w_pg = w_ple_gate[i].astype(BF16)
        w_pp = w_ple_proj[i].astype(BF16)
        j = i // 2
        if i % 2 == 0:
            h1, hn = _outproj_norm(h, hcat, w_out[i].astype(BF16), g_ffn[i][None, :], BF16)
            y = _dense_ffn(hn, w_ff_gate[j].astype(BF16), w_ff_up[j].astype(BF16),
                           w_ff_down[j].astype(BF16))
            h = _dense_tail(h1, y, p_i, g_ple[i][None, :], w_pg, w_pp, g_final[None, :], last)
        else:
            w_r = jnp.zeros((D_MODEL, LANES), F32).at[:, :N_EXPERTS].set(w_router[j])
            h1, hn, ri, rf, cnt = _outproj_router(h, hcat, w_out[i].astype(BF16), g_ffn[i][None, :], w_r)
            keys_tiles, tile_expert, n_tiles = _expert_tiles(ri[:, :TOP_K], cnt[0, :N_EXPERTS], n)
            ys = _moe_ffn(hn, w_e_gate[j].astype(BF16), w_e_up[j].astype(BF16), w_e_down[j].astype(BF16),
                          keys_tiles, tile_expert, n_tiles)
            h = _combine(h1, rf, ys, p_i, g_ple[i][None, :], w_pg, w_pp, g_final[None, :], last)
    return h.reshape(batch, seq, D_MODEL)
```

```python
import functools

import jax
import jax.numpy as jnp
from jax import lax
from jax.experimental import pallas as pl
from jax.experimental.pallas import tpu as pltpu

F32 = jnp.float32
BF16 = jnp.bfloat16

D_MODEL = 1024
NH_M = 4
DH_M = 128
D_M = NH_M * DH_M
D_R = D_MODEL - D_M
NB_R = 8
DB_R = D_R // NB_R
CONV_W = 4
LRU_C = 8.0
D_FF = 2816
N_EXPERTS = 8
TOP_K = 2
D_PLE = 256
EPS = 1e-6

LANES = 128
SUBLANES = 8
V7X_VMEM_BYTES = 64 * 1024 * 1024

ROW_TILE = 512
MIX_CHUNK = 128
FFN_ROW_TILE = 512
FFN_COL_CHUNK = 256


def _vmem_limit(nbytes):
    return int(min(nbytes, V7X_VMEM_BYTES - 6 * 1024 * 1024))


def _rmsnorm(x, g):
    return x * lax.rsqrt(jnp.mean(x * x, axis=-1, keepdims=True) + EPS) * g


def _sigmoid(x):
    return 1.0 / (1.0 + jnp.exp(-x))


def _silu(x):
    return x * _sigmoid(x)


def _inproj_kernel(x_ref, g_ref, w_ref, b_ref, wg_ref, bg_ref,
                   zqk_ref, v_ref, o_ref, zxr_ref, yr_ref, gt_ref):
    xn = _rmsnorm(x_ref[...], g_ref[...]).astype(BF16)

    def seg(lo, hi):
        return jnp.dot(xn, w_ref[:, lo:hi], preferred_element_type=F32) + b_ref[:, lo:hi]

    zqk_ref[...] = seg(0, 2 * D_M)
    v_ref[...] = seg(2 * D_M, 3 * D_M).astype(BF16)
    o_ref[...] = seg(3 * D_M, 4 * D_M)
    zxr_ref[...] = seg(4 * D_M, 4 * D_M + D_R)
    yr_ref[...] = seg(4 * D_M + D_R, 4 * D_M + 2 * D_R)
    gt_ref[...] = lax.dot_general(wg_ref[...], xn, (((1,), (1,)), ((), ())),
                                  preferred_element_type=F32) + bg_ref[...]


def _norm_inproj(h, g, w_main, b_main, w_gt, b_gt):
    n = h.shape[0]
    tm = ROW_TILE
    wcols = w_main.shape[1]
    row = lambda i: (i, 0)
    const = lambda i: (0, 0)
    return pl.pallas_call(
        _inproj_kernel,
        grid=(n // tm,),
        in_specs=[
            pl.BlockSpec((tm, D_MODEL), row),
            pl.BlockSpec((1, D_MODEL), const),
            pl.BlockSpec((D_MODEL, wcols), const),
            pl.BlockSpec((1, wcols), const),
            pl.BlockSpec((2 * NH_M, D_MODEL), const),
            pl.BlockSpec((2 * NH_M, 1), const),
        ],
        out_specs=[
            pl.BlockSpec((tm, 2 * D_M), row),
            pl.BlockSpec((tm, D_M), row),
            pl.BlockSpec((tm, D_M), row),
            pl.BlockSpec((tm, D_R), row),
            pl.BlockSpec((tm, D_R), row),
            pl.BlockSpec((2 * NH_M, tm), lambda i: (0, i)),
        ],
        out_shape=[
            jax.ShapeDtypeStruct((n, 2 * D_M), F32),
            jax.ShapeDtypeStruct((n, D_M), BF16),
            jax.ShapeDtypeStruct((n, D_M), F32),
            jax.ShapeDtypeStruct((n, D_R), F32),
            jax.ShapeDtypeStruct((n, D_R), F32),
            jax.ShapeDtypeStruct((2 * NH_M, n), F32),
        ],
        compiler_params=pltpu.CompilerParams(
            dimension_semantics=("arbitrary",), vmem_limit_bytes=_vmem_limit(48 << 20)),
        name="norm_inproj",
    )(h, g, w_main, b_main, w_gt, b_gt)


def _causal_conv(x, prev, w, b):
    t = x.shape[0]
    xe = jnp.concatenate([prev, x], axis=0)
    acc = xe[SUBLANES:] * w[CONV_W - 1:CONV_W, :] + b
    for j in range(CONV_W - 1):
        shifted = pltpu.roll(xe, CONV_W - 1 - j, axis=0)[SUBLANES:]
        acc = acc + shifted * w[j:j + 1, :]
    del t
    return acc


def _mixer_kernel(zqk_ref, v_ref, o_ref, zxr_ref, yr_ref, gt_ref,
                  wcqk_ref, bcqk_ref, gmh_ref, wcr_ref, bcr_ref, wbd_ref, bbd_ref,
                  lam_ref, gr_ref, ones_ref,
                  out_ref,
                  pqk_ref, pxr_ref, s_ref, m_ref, hr_ref):
    tm = zqk_ref.shape[0]
    first = pl.program_id(1) == 0

    @pl.when(first)
    def _():
        pqk_ref[...] = jnp.zeros_like(pqk_ref)
        pxr_ref[...] = jnp.zeros_like(pxr_ref)
        s_ref[...] = jnp.zeros_like(s_ref)
        m_ref[...] = jnp.zeros_like(m_ref)
        hr_ref[...] = jnp.zeros_like(hr_ref)

    zqk = zqk_ref[...]
    qk = _silu(_causal_conv(zqk, pqk_ref[...], wcqk_ref[...], bcqk_ref[...]))
    pqk_ref[...] = zqk[tm - SUBLANES:, :]

    gt = gt_ref[...]
    logf = jnp.minimum(gt, 0.0) - jnp.log(1.0 + jnp.exp(-jnp.abs(gt)))
    L = MIX_CHUNK
    ri = lax.broadcasted_iota(jnp.int32, (L, L), 0)
    ci = lax.broadcasted_iota(jnp.int32, (L, L), 1)
    tril = ci <= ri
    upper = (ri <= ci).astype(F32)
    ones_blk = jnp.ones((L, DH_M), BF16)
    scale = DH_M ** -0.5
    neg_inf = -jnp.inf

    b_all = jnp.concatenate(
        [jnp.dot(logf[:, c * L:(c + 1) * L], upper, preferred_element_type=F32,
                 precision=lax.Precision.HIGHEST) for c in range(tm // L)], axis=1)

    for h in range(NH_M):
        s_state = s_ref[h]
        m_prev = m_ref[h]
        for c in range(tm // L):
            r0 = c * L
            q = (qk[r0:r0 + L, h * DH_M:(h + 1) * DH_M] * scale).astype(BF16)
            k32 = qk[r0:r0 + L, D_M + h * DH_M:D_M + (h + 1) * DH_M]
            kt32 = k32.T
            v_aug = jnp.concatenate([v_ref[r0:r0 + L, h * DH_M:(h + 1) * DH_M], ones_blk], axis=1)
            ig_row = gt[h:h + 1, r0:r0 + L]
            lf_row = logf[NH_M + h:NH_M + h + 1, r0:r0 + L]
            b_row = b_all[NH_M + h:NH_M + h + 1, r0:r0 + L]
            c_row = ig_row - b_row
            g_col = jnp.maximum(jnp.max(jnp.where(tril, c_row, neg_inf), axis=-1, keepdims=True), m_prev)
            b_col = jnp.sum(jnp.where(tril, lf_row, 0.0), axis=-1, keepdims=True)
            g_last = g_col[L - 1:L, :]
            b_last = b_col[L - 1:L, :]
            dmat = jnp.exp(jnp.where(tril, c_row - g_col, neg_inf))
            s = jnp.dot(q, kt32.astype(BF16), preferred_element_type=F32) * dmat
            inter = jnp.exp(m_prev - g_col)
            out = (jnp.dot(s.astype(BF16), v_aug, preferred_element_type=F32)
                   + inter * jnp.dot(q, s_state.astype(BF16), preferred_element_type=F32))
            num = out[:, :DH_M]
            den = out[:, DH_M:]
            hm = num / jnp.maximum(jnp.abs(den), jnp.exp(-(b_col + g_col)))
            hm = hm * _sigmoid(o_ref[r0:r0 + L, h * DH_M:(h + 1) * DH_M])
            hm = hm * lax.rsqrt(jnp.mean(hm * hm, axis=-1, keepdims=True) + EPS)
            out_ref[r0:r0 + L, h * DH_M:(h + 1) * DH_M] = (
                hm * gmh_ref[:, h * DH_M:(h + 1) * DH_M]).astype(out_ref.dtype)
            wk_row = jnp.exp(c_row - g_last)
            decay = jnp.exp(m_prev - g_last)
            s_state = decay * s_state + jnp.dot((kt32 * wk_row).astype(BF16), v_aug,
                                                preferred_element_type=F32)
            m_prev = b_last + g_last
        s_ref[h] = s_state
        m_ref[h] = m_prev

    zxr = zxr_ref[...]
    xr = _causal_conv(zxr, pxr_ref[...], wcr_ref[...], bcr_ref[...])
    pxr_ref[...] = zxr[tm - SUBLANES:, :]
    gates = _sigmoid(jnp.dot(xr.astype(BF16), wbd_ref[...], preferred_element_type=F32) + bbd_ref[...])
    r_gate = gates[:, :D_R]
    i_gate = gates[:, D_R:]
    nlam = -lam_ref[...]
    softplus = jnp.maximum(nlam, 0.0) + jnp.log(1.0 + jnp.exp(-jnp.abs(nlam)))
    log_a = (-LRU_C) * r_gate * softplus
    a = jnp.exp(log_a)
    u = jnp.sqrt(1.0 - jnp.exp(2.0 * log_a)) * (i_gate * xr)
    row8 = lax.broadcasted_iota(jnp.int32, (tm, D_R), 0) & (SUBLANES - 1)
    for sft in (1, 2, 4):
        keep = row8 >= sft
        a_sh = pltpu.roll(a, sft, axis=0)
        u_sh = pltpu.roll(u, sft, axis=0)
        u = jnp.where(keep, a * u_sh + u, u)
        a = jnp.where(keep, a * a_sh, a)
    hprev = hr_ref[...]
    rows = []
    for gi in range(tm // SUBLANES):
        blk = u[gi * SUBLANES:(gi + 1) * SUBLANES] + a[gi * SUBLANES:(gi + 1) * SUBLANES] * hprev
        rows.append(blk)
        hprev = blk[SUBLANES - 1:SUBLANES, :]
    hr_ref[...] = hprev
    hseq = jnp.concatenate(rows, axis=0)
    yr = yr_ref[...]
    gelu = 0.5 * yr * (1.0 + jnp.tanh(0.7978845608028654 * (yr + 0.044715 * yr * yr * yr)))
    hr = hseq * gelu
    sq = hr * hr
    sq_hi = sq.astype(BF16)
    sq_lo = (sq - sq_hi.astype(F32)).astype(BF16)
    gsum = (jnp.dot(sq_hi, ones_ref[...], preferred_element_type=F32)
            + jnp.dot(sq_lo, ones_ref[...], preferred_element_type=F32))
    hr = hr * lax.rsqrt(gsum * (1.0 / DB_R) + EPS) * gr_ref[...]
    out_ref[:, D_M:] = hr.astype(out_ref.dtype)


def _mixer(zqk, v, o, zxr, yr, gt, wcqk, bcqk, gmh, wcr, bcr, wbd, bbd, lam, gr, ones_bd,
           batch, seq):
    tm = ROW_TILE
    spb = seq // tm
    row = lambda b, s: (b * spb + s, 0)
    const = lambda b, s: (0, 0)
    n = batch * seq
    return pl.pallas_call(
        _mixer_kernel,
        grid=(batch, spb),
        in_specs=[
            pl.BlockSpec((tm, 2 * D_M), row),
            pl.BlockSpec((tm, D_M), row),
            pl.BlockSpec((tm, D_M), row),
            pl.BlockSpec((tm, D_R), row),
            pl.BlockSpec((tm, D_R), row),
            pl.BlockSpec((2 * NH_M, tm), lambda b, s: (0, b * spb + s)),
            pl.BlockSpec((CONV_W, 2 * D_M), const),
            pl.BlockSpec((1, 2 * D_M), const),
            pl.BlockSpec((1, D_M), const),
            pl.BlockSpec((CONV_W, D_R), const),
            pl.BlockSpec((1, D_R), const),
            pl.BlockSpec((D_R, 2 * D_R), const),
            pl.BlockSpec((1, 2 * D_R), const),
            pl.BlockSpec((1, D_R), const),
            pl.BlockSpec((1, D_R), const),
            pl.BlockSpec((D_R, D_R), const),
        ],
        out_specs=pl.BlockSpec((tm, D_MODEL), row),
        out_shape=jax.ShapeDtypeStruct((n, D_MODEL), BF16),
        scratch_shapes=[
            pltpu.VMEM((SUBLANES, 2 * D_M), F32),
            pltpu.VMEM((SUBLANES, D_R), F32),
            pltpu.VMEM((NH_M, DH_M, 2 * DH_M), F32),
            pltpu.VMEM((NH_M, 1, 1), F32),
            pltpu.VMEM((1, D_R), F32),
        ],
        compiler_params=pltpu.CompilerParams(
            dimension_semantics=("arbitrary", "arbitrary"), vmem_limit_bytes=_vmem_limit(48 << 20)),
        name="seq_mixer",
    )(zqk, v, o, zxr, yr, gt, wcqk, bcqk, gmh, wcr, bcr, wbd, bbd, lam, gr, ones_bd)


def _outproj_kernel(h_ref, hc_ref, w_ref, g_ref, h1_ref, hn_ref):
    h1 = h_ref[...] + jnp.dot(hc_ref[...], w_ref[...], preferred_element_type=F32)
    h1_ref[...] = h1
    hn_ref[...] = _rmsnorm(h1, g_ref[...]).astype(hn_ref.dtype)


def _outproj_norm(h, hcat, w_out, g, hn_dtype):
    n = h.shape[0]
    tm = ROW_TILE
    row = lambda i: (i, 0)
    const = lambda i: (0, 0)
    return pl.pallas_call(
        _outproj_kernel,
        grid=(n // tm,),
        in_specs=[
            pl.BlockSpec((tm, D_MODEL), row),
            pl.BlockSpec((tm, D_MODEL), row),
            pl.BlockSpec((D_MODEL, D_MODEL), const),
            pl.BlockSpec((1, D_MODEL), const),
        ],
        out_specs=[pl.BlockSpec((tm, D_MODEL), row), pl.BlockSpec((tm, D_MODEL), row)],
        out_shape=[jax.ShapeDtypeStruct((n, D_MODEL), F32), jax.ShapeDtypeStruct((n, D_MODEL), hn_dtype)],
        compiler_params=pltpu.CompilerParams(
            dimension_semantics=("arbitrary",), vmem_limit_bytes=_vmem_limit(32 << 20)),
        name="outproj_norm",
    )(h, hcat, w_out, g)


def _outproj_router_kernel(h_ref, hc_ref, w_ref, g_ref, wr_ref, h1_ref, hn_ref, ri_ref, rf_ref, cnt_ref,
                           carry_ref):
    tm = h_ref.shape[0]

    @pl.when(pl.program_id(0) == 0)
    def _():
        carry_ref[...] = jnp.zeros_like(carry_ref)

    h1 = h_ref[...] + jnp.dot(hc_ref[...], w_ref[...], preferred_element_type=F32)
    h1_ref[...] = h1
    hn = _rmsnorm(h1, g_ref[...])
    _store_rows_3d(hn_ref, hn)
    lane_i = lax.broadcasted_iota(jnp.int32, (tm, LANES), 1)
    lane = lane_i.astype(F32)
    hn_hi = hn.astype(BF16)
    hn_lo = (hn - hn_hi.astype(F32)).astype(BF16)
    wr = wr_ref[...]
    wr_hi = wr.astype(BF16)
    wr_lo = (wr - wr_hi.astype(F32)).astype(BF16)
    logits = (jnp.dot(hn_hi, wr_hi, preferred_element_type=F32)
              + (jnp.dot(hn_hi, wr_lo, preferred_element_type=F32)
                 + jnp.dot(hn_lo, wr_hi, preferred_element_type=F32)))
    logits = jnp.where(lane_i < N_EXPERTS, logits, -jnp.inf)
    v1 = jnp.max(logits, axis=-1, keepdims=True)
    i1 = jnp.min(jnp.where(logits == v1, lane, float(LANES)), axis=-1, keepdims=True)
    m1 = lane == i1
    rest = jnp.where(m1, -jnp.inf, logits)
    v2 = jnp.max(rest, axis=-1, keepdims=True)
    i2 = jnp.min(jnp.where(rest == v2, lane, float(LANES)), axis=-1, keepdims=True)
    m2 = lane == i2
    e2 = jnp.exp(v2 - v1)
    g1 = 1.0 / (1.0 + e2)
    g2 = e2 / (1.0 + e2)
    carry = carry_ref[...] + jnp.sum(jnp.where(m1 | m2, 1.0, 0.0), axis=0, keepdims=True)
    carry_ref[...] = carry
    ri_ref[...] = jnp.where(lane_i == 0, i1.astype(jnp.int32), i2.astype(jnp.int32))
    rf_ref[...] = jnp.where(lane_i == 0, g1, g2)
    cnt_ref[...] = jnp.broadcast_to(carry, cnt_ref.shape)


def _outproj_router(h, hcat, w_out, g, w_router_pad):
    n = h.shape[0]
    tm = ROW_TILE
    row = lambda i: (i, 0)
    const = lambda i: (0, 0)
    return pl.pallas_call(
        _outproj_router_kernel,
        grid=(n // tm,),
        in_specs=[
            pl.BlockSpec((tm, D_MODEL), row),
            pl.BlockSpec((tm, D_MODEL), row),
            pl.BlockSpec((D_MODEL, D_MODEL), const),
            pl.BlockSpec((1, D_MODEL), const),
            pl.BlockSpec((D_MODEL, LANES), const),
        ],
        out_specs=[pl.BlockSpec((tm, D_MODEL), row),
                   pl.BlockSpec((tm, ROW_SLABS, LANES), lambda i: (i, 0, 0)),
                   pl.BlockSpec((tm, LANES), row), pl.BlockSpec((tm, LANES), row),
                   pl.BlockSpec((SUBLANES, LANES), const)],
        out_shape=[jax.ShapeDtypeStruct((n, D_MODEL), F32),
                   jax.ShapeDtypeStruct((n, ROW_SLABS, LANES), F32),
                   jax.ShapeDtypeStruct((n, LANES), jnp.int32), jax.ShapeDtypeStruct((n, LANES), F32),
                   jax.ShapeDtypeStruct((SUBLANES, LANES), F32)],
        scratch_shapes=[pltpu.VMEM((1, LANES), F32)],
        compiler_params=pltpu.CompilerParams(
            dimension_semantics=("arbitrary",), vmem_limit_bytes=_vmem_limit(40 << 20)),
        name="outproj_router",
    )(h, hcat, w_out, g, w_router_pad)


def _swiglu(x, wg_ref, wu_ref, wd_ref, act_ref):
    for f0 in range(0, D_FF, FFN_COL_CHUNK):
        g = jnp.dot(x, wg_ref[:, f0:f0 + FFN_COL_CHUNK], preferred_element_type=F32)
        u = jnp.dot(x, wu_ref[:, f0:f0 + FFN_COL_CHUNK], preferred_element_type=F32)
        act_ref[:, f0:f0 + FFN_COL_CHUNK] = (_silu(g) * u).astype(BF16)
    return jnp.dot(act_ref[...], wd_ref[...], preferred_element_type=F32)


def _dense_ffn_kernel(x_ref, wg_ref, wu_ref, wd_ref, y_ref, act_ref):
    y_ref[...] = _swiglu(x_ref[...], wg_ref, wu_ref, wd_ref, act_ref)


def _dense_ffn(x, w_gate, w_up, w_down):
    n = x.shape[0]
    tm = FFN_ROW_TILE
    row = lambda i: (i, 0)
    const = lambda i: (0, 0)
    return pl.pallas_call(
        _dense_ffn_kernel,
        grid=(n // tm,),
        in_specs=[
            pl.BlockSpec((tm, D_MODEL), row),
            pl.BlockSpec((D_MODEL, D_FF), const),
            pl.BlockSpec((D_MODEL, D_FF), const),
            pl.BlockSpec((D_FF, D_MODEL), const),
        ],
        out_specs=pl.BlockSpec((tm, D_MODEL), row),
        out_shape=jax.ShapeDtypeStruct((n, D_MODEL), F32),
        scratch_shapes=[pltpu.VMEM((tm, D_FF), BF16)],
        compiler_params=pltpu.CompilerParams(
            dimension_semantics=("arbitrary",), vmem_limit_bytes=_vmem_limit(58 << 20)),
        name="dense_swiglu",
    )(x, w_gate, w_up, w_down)


KEY_STRIDE = 1 << 16

ROW_SLABS = D_MODEL // LANES


def _rows_to_2d(ref3):
    return jnp.concatenate([ref3[:, s, :] for s in range(ROW_SLABS)], axis=1)


def _store_rows_3d(ref3, val):
    for s in range(ROW_SLABS):
        ref3[:, s, :] = val[:, s * LANES:(s + 1) * LANES]


ROW_COPY_UNROLL = 8


def _moe_ffn_kernel(te_ref, nt_ref, src_ref, src_next_ref, dst_ref, hn_ref, wg_ref, wu_ref, wd_ref,
                    out_ref, xbuf, ybuf, act_ref, sem_g, sem_s, *, n_tok):
    i = pl.program_id(0)
    nt = nt_ref[0]
    tmf = xbuf.shape[1]
    slot = i % 2
    dump0 = TOP_K * n_tok

    def gather(sref, slot_):
        def body(j, carry):
            for u in range(ROW_COPY_UNROLL):
                r = ROW_COPY_UNROLL * j + u
                pltpu.make_async_copy(hn_ref.at[sref[0, 0, r]], xbuf.at[slot_, r],
                                      sem_g.at[slot_]).start(priority=u % 2)
            return carry
        lax.fori_loop(0, tmf // ROW_COPY_UNROLL, body, 0)

    def wait_rows(buf, sem):
        pltpu.make_async_copy(hn_ref.at[pl.ds(0, tmf)], buf, sem).wait()

    @pl.when(i == 0)
    def _():
        ybuf[...] = jnp.zeros_like(ybuf)
        for s in range(2):
            fill = pltpu.make_async_copy(ybuf.at[s], out_ref.at[pl.ds(dump0 + s * tmf, tmf)], sem_s.at[s])
            fill.start()
            fill.wait()
        gather(src_ref, 0)

    @pl.when(i + 1 < nt)
    def _():
        gather(src_next_ref, 1 - slot)

    @pl.when(i < nt)
    def _():
        wait_rows(xbuf.at[slot], sem_g.at[slot])
        y = _swiglu(_rows_to_2d(xbuf.at[slot]).astype(BF16), wg_ref, wu_ref, wd_ref, act_ref)

        @pl.when(i >= 2)
        def _():
            wait_rows(ybuf.at[slot], sem_s.at[slot])

        _store_rows_3d(ybuf.at[slot], y)

        def body(j, carry):
            for u in range(ROW_COPY_UNROLL):
                r = ROW_COPY_UNROLL * j + u
                pltpu.make_async_copy(ybuf.at[slot, r], out_ref.at[dst_ref[0, 0, r]],
                                      sem_s.at[slot]).start(priority=u % 2)
            return carry
        lax.fori_loop(0, tmf // ROW_COPY_UNROLL, body, 0)

    @pl.when(i == pl.num_programs(0) - 1)
    def _():
        @pl.when(nt >= 2)
        def _():
            wait_rows(ybuf.at[nt % 2], sem_s.at[nt % 2])
        wait_rows(ybuf.at[(nt - 1) % 2], sem_s.at[(nt - 1) % 2])


def _moe_ffn(hn, w_gate, w_up, w_down, src_tiles, dst_tiles, tile_expert, n_tiles):
    n_tok = hn.shape[0]
    tmf = FFN_ROW_TILE
    t_total = src_tiles.shape[0]
    grid_spec = pltpu.PrefetchScalarGridSpec(
        num_scalar_prefetch=2,
        grid=(t_total,),
        in_specs=[
            pl.BlockSpec((1, 1, tmf), lambda i, te, nt: (i, 0, 0), memory_space=pltpu.SMEM),
            pl.BlockSpec((1, 1, tmf), lambda i, te, nt: (jnp.minimum(i + 1, t_total - 1), 0, 0),
                         memory_space=pltpu.SMEM),
            pl.BlockSpec((1, 1, tmf), lambda i, te, nt: (i, 0, 0), memory_space=pltpu.SMEM),
            pl.BlockSpec(memory_space=pl.ANY),
            pl.BlockSpec((None, D_MODEL, D_FF), lambda i, te, nt: (te[i], 0, 0)),
            pl.BlockSpec((None, D_MODEL, D_FF), lambda i, te, nt: (te[i], 0, 0)),
            pl.BlockSpec((None, D_FF, D_MODEL), lambda i, te, nt: (te[i], 0, 0)),
        ],
        out_specs=pl.BlockSpec(memory_space=pl.ANY),
        scratch_shapes=[
            pltpu.VMEM((2, tmf, ROW_SLABS, LANES), F32),
            pltpu.VMEM((2, tmf, ROW_SLABS, LANES), F32),
            pltpu.VMEM((tmf, D_FF), BF16),
            pltpu.SemaphoreType.DMA((2,)),
            pltpu.SemaphoreType.DMA((2,)),
        ],
    )
    return pl.pallas_call(
        functools.partial(_moe_ffn_kernel, n_tok=n_tok),
        grid_spec=grid_spec,
        out_shape=jax.ShapeDtypeStruct((TOP_K * n_tok + 2 * tmf, ROW_SLABS, LANES), F32),
        compiler_params=pltpu.CompilerParams(
            dimension_semantics=("arbitrary",), vmem_limit_bytes=_vmem_limit(58 << 20),
            has_side_effects=True),
        name="expert_swiglu",
    )(tile_expert, n_tiles, src_tiles, src_tiles, dst_tiles, hn, w_gate, w_up, w_down)


def _ple_tail(h2, p_tile, g_ple, wpg_ref, wpp_ref):
    gate = _sigmoid(jnp.dot(_rmsnorm(h2, g_ple).astype(BF16), wpg_ref[...], preferred_element_type=F32))
    return h2 + gate * jnp.dot(p_tile.astype(BF16), wpp_ref[...], preferred_element_type=F32)


def _combine_kernel(h1_ref, rf_ref, y0_ref, y1_ref, p_ref, gple_ref, wpg_ref, wpp_ref, gfin_ref,
                    out_ref, *, final_norm):
    rf = rf_ref[...]
    h2 = h1_ref[...] + rf[:, 0:1] * _rows_to_2d(y0_ref) + rf[:, 1:2] * _rows_to_2d(y1_ref)
    h3 = _ple_tail(h2, p_ref[...], gple_ref[...], wpg_ref, wpp_ref)
    out_ref[...] = _rmsnorm(h3, gfin_ref[...]) if final_norm else h3


def _combine(h1, rf, ys, p, g_ple, w_pg, w_pp, g_final, final_norm):
    n = h1.shape[0]
    tm = ROW_TILE
    row = lambda i: (i, 0)
    const = lambda i: (0, 0)
    return pl.pallas_call(
        functools.partial(_combine_kernel, final_norm=final_norm),
        grid=(n // tm,),
        in_specs=[
            pl.BlockSpec((tm, D_MODEL), row),
            pl.BlockSpec((tm, LANES), row),
            pl.BlockSpec((tm, ROW_SLABS, LANES), lambda i: (i, 0, 0)),
            pl.BlockSpec((tm, ROW_SLABS, LANES), lambda i: (n // tm + i, 0, 0)),
            pl.BlockSpec((tm, D_PLE), row),
            pl.BlockSpec((1, D_MODEL), const),
            pl.BlockSpec((D_MODEL, D_MODEL), const),
            pl.BlockSpec((D_PLE, D_MODEL), const),
            pl.BlockSpec((1, D_MODEL), const),
        ],
        out_specs=pl.BlockSpec((tm, D_MODEL), row),
        out_shape=jax.ShapeDtypeStruct((n, D_MODEL), F32),
        compiler_params=pltpu.CompilerParams(
            dimension_semantics=("arbitrary",), vmem_limit_bytes=_vmem_limit(40 << 20)),
        name="expert_combine_tail",
    )(h1, rf, ys, ys, p, g_ple, w_pg, w_pp, g_final)


def _dense_tail_kernel(h1_ref, y_ref, p_ref, gple_ref, wpg_ref, wpp_ref, gfin_ref, out_ref, *,
                       final_norm):
    h2 = h1_ref[...] + y_ref[...]
    h3 = _ple_tail(h2, p_ref[...], gple_ref[...], wpg_ref, wpp_ref)
    out_ref[...] = _rmsnorm(h3, gfin_ref[...]) if final_norm else h3


def _dense_tail(h1, y, p, g_ple, w_pg, w_pp, g_final, final_norm):
    n = h1.shape[0]
    tm = ROW_TILE
    row = lambda i: (i, 0)
    const = lambda i: (0, 0)
    return pl.pallas_call(
        functools.partial(_dense_tail_kernel, final_norm=final_norm),
        grid=(n // tm,),
        in_specs=[
            pl.BlockSpec((tm, D_MODEL), row),
            pl.BlockSpec((tm, D_MODEL), row),
            pl.BlockSpec((tm, D_PLE), row),
            pl.BlockSpec((1, D_MODEL), const),
            pl.BlockSpec((D_MODEL, D_MODEL), const),
            pl.BlockSpec((D_PLE, D_MODEL), const),
            pl.BlockSpec((1, D_MODEL), const),
        ],
        out_specs=pl.BlockSpec((tm, D_MODEL), row),
        out_shape=jax.ShapeDtypeStruct((n, D_MODEL), F32),
        compiler_params=pltpu.CompilerParams(
            dimension_semantics=("arbitrary",), vmem_limit_bytes=_vmem_limit(32 << 20)),
        name="dense_ple_tail",
    )(h1, y, p, g_ple, w_pg, w_pp, g_final)


def _block_diag(w):
    nb, db, _ = w.shape
    eye = jnp.eye(nb, dtype=w.dtype)
    return (eye[:, None, :, None] * w[:, :, None, :]).reshape(nb * db, nb * db)


def _mixer_layer(h, g_mix, w_in, b_in, w_conv_qk, b_conv_qk, g_mh, w_conv_r, b_conv_r,
                 w_ra, b_ra, w_ri, b_ri, lam, g_r, batch, seq):
    q_end, k_end, v_end, o_end = D_M, 2 * D_M, 3 * D_M, 4 * D_M
    i_end = o_end + NH_M
    f_end = i_end + NH_M
    xr_end = f_end + D_R
    w_main = jnp.concatenate([w_in[:, :o_end], w_in[:, f_end:]], axis=1).astype(BF16)
    b_main = jnp.concatenate([b_in[:o_end], b_in[f_end:]])[None, :]
    w_gt = w_in[:, o_end:f_end].T.astype(BF16)
    b_gt = b_in[o_end:f_end][:, None]
    del q_end, k_end, v_end, xr_end
    zqk, v, o, zxr, yr, gt = _norm_inproj(h, g_mix[None, :], w_main, b_main, w_gt, b_gt)
    wbd = jnp.concatenate([_block_diag(w_ra), _block_diag(w_ri)], axis=1).astype(BF16)
    bbd = jnp.concatenate([b_ra, b_ri])[None, :]
    ones_bd = _block_diag(jnp.ones((NB_R, DB_R, DB_R), F32)).astype(BF16)
    return _mixer(zqk, v, o, zxr, yr, gt, w_conv_qk, b_conv_qk[None, :], g_mh[None, :],
                  w_conv_r, b_conv_r[None, :], wbd, bbd, lam[None, :], g_r[None, :], ones_bd,
                  batch, seq)


def _expert_tiles(top_idx, counts_f, n_tok):
    tmf = FFN_ROW_TILE
    counts = counts_f.astype(jnp.int32)
    padded = (counts + tmf - 1) // tmf * tmf
    n_tiles = (jnp.sum(padded) // tmf).astype(jnp.int32).reshape(1)
    a_ids = jnp.arange(TOP_K * n_tok, dtype=jnp.int32)
    real = top_idx.reshape(-1) * KEY_STRIDE + a_ids
    pad_j = jnp.arange(tmf, dtype=jnp.int32)[None, :]
    experts = jnp.arange(N_EXPERTS, dtype=jnp.int32)[:, None]
    pad_keys = jnp.where(pad_j < (padded - counts)[:, None],
                         experts * KEY_STRIDE + TOP_K * n_tok + pad_j, N_EXPERTS * KEY_STRIDE)
    keys = jnp.sort(jnp.concatenate([real, pad_keys.reshape(-1)]))
    t_total = (TOP_K * n_tok) // tmf + N_EXPERTS
    a_sorted = keys & (KEY_STRIDE - 1)
    valid = a_sorted < TOP_K * n_tok
    row = jnp.arange(t_total * tmf, dtype=jnp.int32)
    dump = TOP_K * n_tok + (row // tmf % 2) * tmf + row % tmf
    src = jnp.where(valid, a_sorted >> 1, 0).reshape(t_total, 1, tmf)
    dst = jnp.where(valid, (a_sorted & 1) * n_tok + (a_sorted >> 1), dump).reshape(t_total, 1, tmf)
    tile_expert = jnp.minimum(keys[::tmf] // KEY_STRIDE, N_EXPERTS - 1).astype(jnp.int32)
    return src, dst, tile_expert, n_tiles


def kernel(x, p, g_mix, w_in, b_in, w_conv_qk, b_conv_qk, g_mh, w_conv_r, b_conv_r, w_ra, b_ra, w_ri, b_ri, lam, g_r, w_out, g_ffn, w_ff_gate, w_ff_up, w_ff_down, w_router, w_e_gate, w_e_up, w_e_down, g_ple, w_ple_gate, w_ple_proj, g_final):
    batch, seq, _ = x.shape
    n = batch * seq
    depth = g_mix.shape[0]
    h = x.reshape(n, D_MODEL)
    tmf = FFN_ROW_TILE
    for i in range(depth):
        last = i == depth - 1
        hcat = _mixer_layer(h, g_mix[i], w_in[i], b_in[i], w_conv_qk[i], b_conv_qk[i], g_mh[i],
                            w_conv_r[i], b_conv_r[i], w_ra[i], b_ra[i], w_ri[i], b_ri[i], lam[i],
                            g_r[i], batch, seq)
        p_i = p[i].reshape(n, D_PLE)
        w_pg = w_ple_gate[i].astype(BF16)
        w_pp = w_ple_proj[i].astype(BF16)
        j = i // 2
        if i % 2 == 0:
            h1, hn = _outproj_norm(h, hcat, w_out[i].astype(BF16), g_ffn[i][None, :], BF16)
            y = _dense_ffn(hn, w_ff_gate[j].astype(BF16), w_ff_up[j].astype(BF16),
                           w_ff_down[j].astype(BF16))
            h = _dense_tail(h1, y, p_i, g_ple[i][None, :], w_pg, w_pp, g_final[None, :], last)
        else:
            w_r = jnp.zeros((D_MODEL, LANES), F32).at[:, :N_EXPERTS].set(w_router[j])
            h1, hn, ri, rf, cnt = _outproj_router(h, hcat, w_out[i].astype(BF16), g_ffn[i][None, :], w_r)
            src, dst, tile_expert, n_tiles = _expert_tiles(ri[:, :TOP_K], cnt[0, :N_EXPERTS], n)
            ys = _moe_ffn(hn, w_e_gate[j].astype(BF16), w_e_up[j].astype(BF16), w_e_down[j].astype(BF16),
                          src, dst, tile_expert, n_tiles)
            h = _combine(h1, rf, ys, p_i, g_ple[i][None, :], w_pg, w_pp, g_final[None, :], last)
    return h.reshape(batch, seq, D_MODEL)
```

```python
import functools

import jax
import jax.numpy as jnp
from jax import lax
from jax.experimental import pallas as pl
from jax.experimental.pallas import tpu as pltpu

F32 = jnp.float32
BF16 = jnp.bfloat16

D_MODEL = 1024
NH_M = 4
DH_M = 128
D_M = NH_M * DH_M
D_R = D_MODEL - D_M
NB_R = 8
DB_R = D_R // NB_R
CONV_W = 4
LRU_C = 8.0
D_FF = 2816
N_EXPERTS = 8
TOP_K = 2
D_PLE = 256
EPS = 1e-6

LANES = 128
SUBLANES = 8
V7X_VMEM_BYTES = 64 * 1024 * 1024

ROW_TILE = 512
MIX_CHUNK = 128
FFN_ROW_TILE = 512
FFN_COL_CHUNK = 256
INPROJ_COL_CHUNK = 256


def _vmem_limit(nbytes):
    return int(min(nbytes, V7X_VMEM_BYTES - 6 * 1024 * 1024))


def _rmsnorm(x, g):
    return x * lax.rsqrt(jnp.mean(x * x, axis=-1, keepdims=True) + EPS) * g


def _sigmoid(x):
    return 1.0 / (1.0 + jnp.exp(-x))


def _silu(x):
    return x * _sigmoid(x)


def _causal_conv(x, prev, w, b, stage_ref):
    t = x.shape[0]
    stage_ref[0:SUBLANES, :] = prev
    stage_ref[SUBLANES:, :] = x
    acc = x * w[CONV_W - 1:CONV_W, :] + b
    for j in range(CONV_W - 1):
        lo = SUBLANES - (CONV_W - 1) + j
        acc = acc + stage_ref[lo:lo + t, :] * w[j:j + 1, :]
    return acc


def _inproj_kernel(x_ref, g_ref, w_ref, b_ref, wg_ref, bg_ref, wcqk_ref, bcqk_ref, wcr_ref, bcr_ref,
                   wbd_ref, bbd_ref, lam_ref,
                   q_ref, k_ref, v_ref, og_ref, a_ref, u_ref, gy_ref, gt_ref,
                   pqk_ref, pxr_ref, xr_ref, cstage_ref, *, tiles_per_seq):
    tm = x_ref.shape[0]

    @pl.when(pl.program_id(0) % tiles_per_seq == 0)
    def _():
        pqk_ref[...] = jnp.zeros_like(pqk_ref)
        pxr_ref[...] = jnp.zeros_like(pxr_ref)

    xn = _rmsnorm(x_ref[...], g_ref[...]).astype(BF16)
    gt_ref[...] = lax.dot_general(wg_ref[...], xn, (((1,), (1,)), ((), ())),
                                  preferred_element_type=F32) + bg_ref[...]
    cw = INPROJ_COL_CHUNK

    n_stage = [0]

    def next_stage():
        n_stage[0] += 1
        return cstage_ref.at[n_stage[0] % 2]

    def post_qk(z, c0):
        cols = slice(c0, c0 + cw)
        y = _silu(_causal_conv(z, pqk_ref[:, cols], wcqk_ref[:, cols], bcqk_ref[:, cols], next_stage()))
        pqk_ref[:, cols] = z[tm - SUBLANES:, :]
        if c0 < D_M:
            q_ref[:, cols] = (y * (DH_M ** -0.5)).astype(BF16)
        else:
            k_ref[:, c0 - D_M:c0 - D_M + cw] = y.astype(BF16)

    def post_v(z, c0):
        v_ref[:, c0:c0 + cw] = z.astype(BF16)

    def post_og(z, c0):
        og_ref[:, c0:c0 + cw] = _sigmoid(z).astype(BF16)

    def post_xr(z, c0):
        cols = slice(c0, c0 + cw)
        xr_ref[:, cols] = _causal_conv(z, pxr_ref[:, cols], wcr_ref[:, cols], bcr_ref[:, cols],
                                       next_stage())
        pxr_ref[:, cols] = z[tm - SUBLANES:, :]

    def post_gy(z, c0):
        gy_ref[:, c0:c0 + cw] = (
            0.5 * z * (1.0 + jnp.tanh(0.7978845608028654 * (z + 0.044715 * z * z * z)))).astype(BF16)

    stages = []
    for base, width, post in ((0, 2 * D_M, post_qk), (2 * D_M, D_M, post_v), (3 * D_M, D_M, post_og),
                              (4 * D_M, D_R, post_xr), (4 * D_M + D_R, D_R, post_gy)):
        stages += [(base + c0, c0, post) for c0 in range(0, width, cw)]

    def project(col):
        return jnp.dot(xn, w_ref[:, col:col + cw], preferred_element_type=F32) + b_ref[:, col:col + cw]

    z_next = project(stages[0][0])
    for idx, (_, c0, post) in enumerate(stages):
        z = z_next
        if idx + 1 < len(stages):
            z_next = project(stages[idx + 1][0])
        post(z, c0)

    xr_bf = xr_ref[...].astype(BF16)
    nlam = -lam_ref[...]
    softplus = jnp.maximum(nlam, 0.0) + jnp.log(1.0 + jnp.exp(-jnp.abs(nlam)))
    for c0 in range(0, D_R, cw):
        cols = slice(c0, c0 + cw)
        icols = slice(D_R + c0, D_R + c0 + cw)
        r_gate = _sigmoid(jnp.dot(xr_bf, wbd_ref[:, cols], preferred_element_type=F32) + bbd_ref[:, cols])
        i_gate = _sigmoid(jnp.dot(xr_bf, wbd_ref[:, icols], preferred_element_type=F32) + bbd_ref[:, icols])
        log_a = (-LRU_C) * r_gate * softplus[:, cols]
        a_ref[:, cols] = jnp.exp(log_a)
        u_ref[:, cols] = jnp.sqrt(1.0 - jnp.exp(2.0 * log_a)) * (i_gate * xr_ref[:, cols])


def _norm_inproj(h, g, w_main, b_main, w_gt, b_gt, wcqk, bcqk, wcr, bcr, wbd, bbd, lam, seq):
    n = h.shape[0]
    tm = ROW_TILE
    wcols = w_main.shape[1]
    row = lambda i: (i, 0)
    const = lambda i: (0, 0)
    return pl.pallas_call(
        functools.partial(_inproj_kernel, tiles_per_seq=seq // tm),
        grid=(n // tm,),
        in_specs=[
            pl.BlockSpec((tm, D_MODEL), row),
            pl.BlockSpec((1, D_MODEL), const),
            pl.BlockSpec((D_MODEL, wcols), const),
            pl.BlockSpec((1, wcols), const),
            pl.BlockSpec((2 * NH_M, D_MODEL), const),
            pl.BlockSpec((2 * NH_M, 1), const),
            pl.BlockSpec((CONV_W, 2 * D_M), const),
            pl.BlockSpec((1, 2 * D_M), const),
            pl.BlockSpec((CONV_W, D_R), const),
            pl.BlockSpec((1, D_R), const),
            pl.BlockSpec((D_R, 2 * D_R), const),
            pl.BlockSpec((1, 2 * D_R), const),
            pl.BlockSpec((1, D_R), const),
        ],
        out_specs=[
            pl.BlockSpec((tm, D_M), row),
            pl.BlockSpec((tm, D_M), row),
            pl.BlockSpec((tm, D_M), row),
            pl.BlockSpec((tm, D_M), row),
            pl.BlockSpec((tm, D_R), row),
            pl.BlockSpec((tm, D_R), row),
            pl.BlockSpec((tm, D_R), row),
            pl.BlockSpec((2 * NH_M, tm), lambda i: (0, i)),
        ],
        out_shape=[
            jax.ShapeDtypeStruct((n, D_M), BF16),
            jax.ShapeDtypeStruct((n, D_M), BF16),
            jax.ShapeDtypeStruct((n, D_M), BF16),
            jax.ShapeDtypeStruct((n, D_M), BF16),
            jax.ShapeDtypeStruct((n, D_R), F32),
            jax.ShapeDtypeStruct((n, D_R), F32),
            jax.ShapeDtypeStruct((n, D_R), BF16),
            jax.ShapeDtypeStruct((2 * NH_M, n), F32),
        ],
        scratch_shapes=[pltpu.VMEM((SUBLANES, 2 * D_M), F32), pltpu.VMEM((SUBLANES, D_R), F32),
                        pltpu.VMEM((tm, D_R), F32),
                        pltpu.VMEM((2, tm + SUBLANES, INPROJ_COL_CHUNK), F32)],
        compiler_params=pltpu.CompilerParams(
            dimension_semantics=("arbitrary",), vmem_limit_bytes=_vmem_limit(48 << 20)),
        name="norm_inproj",
    )(h, g, w_main, b_main, w_gt, b_gt, wcqk, bcqk, wcr, bcr, wbd, bbd, lam)


def _mixer_kernel(q_ref, k_ref, v_ref, og_ref, a_ref, u_ref, gy_ref, gt_ref, gmh_ref, gr_ref, ones_ref,
                  out_ref, s_ref, m_ref, hr_ref):
    tm = q_ref.shape[0]

    @pl.when(pl.program_id(1) == 0)
    def _():
        s_ref[...] = jnp.zeros_like(s_ref)
        m_ref[...] = jnp.zeros_like(m_ref)
        hr_ref[...] = jnp.zeros_like(hr_ref)

    gt = gt_ref[...]
    logf = jnp.minimum(gt, 0.0) - jnp.log(1.0 + jnp.exp(-jnp.abs(gt)))
    L = MIX_CHUNK
    ri = lax.broadcasted_iota(jnp.int32, (L, L), 0)
    ci = lax.broadcasted_iota(jnp.int32, (L, L), 1)
    tril = ci <= ri
    upper = (ri <= ci).astype(F32)
    ones_blk = jnp.ones((L, DH_M), BF16)
    neg_inf = -jnp.inf

    b_all = jnp.concatenate(
        [jnp.dot(logf[:, c * L:(c + 1) * L], upper, preferred_element_type=F32,
                 precision=lax.Precision.HIGHEST) for c in range(tm // L)], axis=1)

    for h in range(NH_M):
        s_state = s_ref[h]
        m_prev = m_ref[h]
        for c in range(tm // L):
            r0 = c * L
            q = q_ref[r0:r0 + L, h * DH_M:(h + 1) * DH_M]
            kt32 = k_ref[r0:r0 + L, h * DH_M:(h + 1) * DH_M].astype(F32).T
            v_aug = jnp.concatenate([v_ref[r0:r0 + L, h * DH_M:(h + 1) * DH_M], ones_blk], axis=1)
            ig_row = gt[h:h + 1, r0:r0 + L]
            lf_row = logf[NH_M + h:NH_M + h + 1, r0:r0 + L]
            b_row = b_all[NH_M + h:NH_M + h + 1, r0:r0 + L]
            c_row = ig_row - b_row
            g_col = jnp.maximum(jnp.max(jnp.where(tril, c_row, neg_inf), axis=-1, keepdims=True), m_prev)
            b_col = jnp.sum(jnp.where(tril, lf_row, 0.0), axis=-1, keepdims=True)
            g_last = g_col[L - 1:L, :]
            b_last = b_col[L - 1:L, :]
            dmat = jnp.exp(jnp.where(tril, c_row - g_col, neg_inf))
            s = jnp.dot(q, kt32.astype(BF16), preferred_element_type=F32) * dmat
            inter = jnp.exp(m_prev - g_col)
            out = (jnp.dot(s.astype(BF16), v_aug, preferred_element_type=F32)
                   + inter * jnp.dot(q, s_state.astype(BF16), preferred_element_type=F32))
            num = out[:, :DH_M]
            den = out[:, DH_M:]
            hm = num / jnp.maximum(jnp.abs(den), jnp.exp(-(b_col + g_col)))
            hm = hm * og_ref[r0:r0 + L, h * DH_M:(h + 1) * DH_M].astype(F32)
            hm = hm * lax.rsqrt(jnp.mean(hm * hm, axis=-1, keepdims=True) + EPS)
            out_ref[r0:r0 + L, h * DH_M:(h + 1) * DH_M] = (
                hm * gmh_ref[:, h * DH_M:(h + 1) * DH_M]).astype(out_ref.dtype)
            wk_row = jnp.exp(c_row - g_last)
            decay = jnp.exp(m_prev - g_last)
            s_state = decay * s_state + jnp.dot((kt32 * wk_row).astype(BF16), v_aug,
                                                preferred_element_type=F32)
            m_prev = b_last + g_last
        s_ref[h] = s_state
        m_ref[h] = m_prev

    groups = tm // SUBLANES
    a = a_ref[...].reshape(groups, SUBLANES, D_R)
    u = u_ref[...].reshape(groups, SUBLANES, D_R)
    row8 = lax.broadcasted_iota(jnp.int32, (groups, SUBLANES, D_R), 1)
    for sft in (1, 2, 4):
        keep = row8 >= sft
        a_sh = pltpu.roll(a, sft, axis=1)
        u_sh = pltpu.roll(u, sft, axis=1)
        u = jnp.where(keep, a * u_sh + u, u)
        a = jnp.where(keep, a * a_sh, a)
    hprev = hr_ref[...]
    rows = []
    for gi in range(groups):
        blk = u[gi] + a[gi] * hprev
        rows.append(blk)
        hprev = blk[SUBLANES - 1:SUBLANES, :]
    hr_ref[...] = hprev
    hr = jnp.concatenate(rows, axis=0) * gy_ref[...].astype(F32)
    sq = hr * hr
    sq_hi = sq.astype(BF16)
    sq_lo = (sq - sq_hi.astype(F32)).astype(BF16)
    gsum = (jnp.dot(sq_hi, ones_ref[...], preferred_element_type=F32)
            + jnp.dot(sq_lo, ones_ref[...], preferred_element_type=F32))
    hr = hr * lax.rsqrt(gsum * (1.0 / DB_R) + EPS) * gr_ref[...]
    out_ref[:, D_M:] = hr.astype(out_ref.dtype)


def _mixer(q, k, v, og, a, u, gy, gt, gmh, gr, ones_bd, batch, seq):
    tm = ROW_TILE
    spb = seq // tm
    row = lambda b, s: (b * spb + s, 0)
    const = lambda b, s: (0, 0)
    n = batch * seq
    return pl.pallas_call(
        _mixer_kernel,
        grid=(batch, spb),
        in_specs=[
            pl.BlockSpec((tm, D_M), row),
            pl.BlockSpec((tm, D_M), row),
            pl.BlockSpec((tm, D_M), row),
            pl.BlockSpec((tm, D_M), row),
            pl.BlockSpec((tm, D_R), row),
            pl.BlockSpec((tm, D_R), row),
            pl.BlockSpec((tm, D_R), row),
            pl.BlockSpec((2 * NH_M, tm), lambda b, s: (0, b * spb + s)),
            pl.BlockSpec((1, D_M), const),
            pl.BlockSpec((1, D_R), const),
            pl.BlockSpec((D_R, D_R), const),
        ],
        out_specs=pl.BlockSpec((tm, D_MODEL), row),
        out_shape=jax.ShapeDtypeStruct((n, D_MODEL), BF16),
        scratch_shapes=[
            pltpu.VMEM((NH_M, DH_M, 2 * DH_M), F32),
            pltpu.VMEM((NH_M, 1, 1), F32),
            pltpu.VMEM((1, D_R), F32),
        ],
        compiler_params=pltpu.CompilerParams(
            dimension_semantics=("arbitrary", "arbitrary"), vmem_limit_bytes=_vmem_limit(48 << 20)),
        name="seq_mixer",
    )(q, k, v, og, a, u, gy, gt, gmh, gr, ones_bd)


def _outproj_kernel(h_ref, hc_ref, w_ref, g_ref, h1_ref, hn_ref):
    h1 = h_ref[...] + jnp.dot(hc_ref[...], w_ref[...], preferred_element_type=F32)
    h1_ref[...] = h1
    hn_ref[...] = _rmsnorm(h1, g_ref[...]).astype(hn_ref.dtype)


def _outproj_norm(h, hcat, w_out, g, hn_dtype):
    n = h.shape[0]
    tm = ROW_TILE
    row = lambda i: (i, 0)
    const = lambda i: (0, 0)
    return pl.pallas_call(
        _outproj_kernel,
        grid=(n // tm,),
        in_specs=[
            pl.BlockSpec((tm, D_MODEL), row),
            pl.BlockSpec((tm, D_MODEL), row),
            pl.BlockSpec((D_MODEL, D_MODEL), const),
            pl.BlockSpec((1, D_MODEL), const),
        ],
        out_specs=[pl.BlockSpec((tm, D_MODEL), row), pl.BlockSpec((tm, D_MODEL), row)],
        out_shape=[jax.ShapeDtypeStruct((n, D_MODEL), F32), jax.ShapeDtypeStruct((n, D_MODEL), hn_dtype)],
        compiler_params=pltpu.CompilerParams(
            dimension_semantics=("arbitrary",), vmem_limit_bytes=_vmem_limit(32 << 20)),
        name="outproj_norm",
    )(h, hcat, w_out, g)


def _outproj_router_kernel(h_ref, hc_ref, w_ref, g_ref, wr_ref, h1_ref, hn_ref, ri_ref, rf_ref, cnt_ref,
                           carry_ref):
    tm = h_ref.shape[0]

    @pl.when(pl.program_id(0) == 0)
    def _():
        carry_ref[...] = jnp.zeros_like(carry_ref)

    h1 = h_ref[...] + jnp.dot(hc_ref[...], w_ref[...], preferred_element_type=F32)
    h1_ref[...] = h1
    hn = _rmsnorm(h1, g_ref[...])
    _store_rows_3d(hn_ref, hn)
    lane_i = lax.broadcasted_iota(jnp.int32, (tm, LANES), 1)
    lane = lane_i.astype(F32)
    hn_hi = hn.astype(BF16)
    hn_lo = (hn - hn_hi.astype(F32)).astype(BF16)
    wr = wr_ref[...]
    wr_hi = wr.astype(BF16)
    wr_lo = (wr - wr_hi.astype(F32)).astype(BF16)
    logits = (jnp.dot(hn_hi, wr_hi, preferred_element_type=F32)
              + (jnp.dot(hn_hi, wr_lo, preferred_element_type=F32)
                 + jnp.dot(hn_lo, wr_hi, preferred_element_type=F32)))
    logits = jnp.where(lane_i < N_EXPERTS, logits, -jnp.inf)
    v1 = jnp.max(logits, axis=-1, keepdims=True)
    i1 = jnp.min(jnp.where(logits == v1, lane, float(LANES)), axis=-1, keepdims=True)
    m1 = lane == i1
    rest = jnp.where(m1, -jnp.inf, logits)
    v2 = jnp.max(rest, axis=-1, keepdims=True)
    i2 = jnp.min(jnp.where(rest == v2, lane, float(LANES)), axis=-1, keepdims=True)
    m2 = lane == i2
    e2 = jnp.exp(v2 - v1)
    g1 = 1.0 / (1.0 + e2)
    g2 = e2 / (1.0 + e2)
    carry = carry_ref[...] + jnp.sum(jnp.where(m1 | m2, 1.0, 0.0), axis=0, keepdims=True)
    carry_ref[...] = carry
    ri_ref[...] = jnp.where(lane_i == 0, i1.astype(jnp.int32), i2.astype(jnp.int32))
    rf_ref[...] = jnp.where(lane_i == 0, g1, g2)
    cnt_ref[...] = jnp.broadcast_to(carry, cnt_ref.shape)


def _outproj_router(h, hcat, w_out, g, w_router_pad):
    n = h.shape[0]
    tm = ROW_TILE
    row = lambda i: (i, 0)
    const = lambda i: (0, 0)
    return pl.pallas_call(
        _outproj_router_kernel,
        grid=(n // tm,),
        in_specs=[
            pl.BlockSpec((tm, D_MODEL), row),
            pl.BlockSpec((tm, D_MODEL), row),
            pl.BlockSpec((D_MODEL, D_MODEL), const),
            pl.BlockSpec((1, D_MODEL), const),
            pl.BlockSpec((D_MODEL, LANES), const),
        ],
        out_specs=[pl.BlockSpec((tm, D_MODEL), row),
                   pl.BlockSpec((tm, ROW_SLABS, LANES), lambda i: (i, 0, 0)),
                   pl.BlockSpec((tm, LANES), row), pl.BlockSpec((tm, LANES), row),
                   pl.BlockSpec((SUBLANES, LANES), const)],
        out_shape=[jax.ShapeDtypeStruct((n, D_MODEL), F32),
                   jax.ShapeDtypeStruct((n, ROW_SLABS, LANES), F32),
                   jax.ShapeDtypeStruct((n, LANES), jnp.int32), jax.ShapeDtypeStruct((n, LANES), F32),
                   jax.ShapeDtypeStruct((SUBLANES, LANES), F32)],
        scratch_shapes=[pltpu.VMEM((1, LANES), F32)],
        compiler_params=pltpu.CompilerParams(
            dimension_semantics=("arbitrary",), vmem_limit_bytes=_vmem_limit(40 << 20)),
        name="outproj_router",
    )(h, hcat, w_out, g, w_router_pad)


def _swiglu(x, wg_ref, wu_ref, wd_ref, act_ref):
    for f0 in range(0, D_FF, FFN_COL_CHUNK):
        g = jnp.dot(x, wg_ref[:, f0:f0 + FFN_COL_CHUNK], preferred_element_type=F32)
        u = jnp.dot(x, wu_ref[:, f0:f0 + FFN_COL_CHUNK], preferred_element_type=F32)
        act_ref[:, f0:f0 + FFN_COL_CHUNK] = (_silu(g) * u).astype(BF16)
    return jnp.dot(act_ref[...], wd_ref[...], preferred_element_type=F32)


def _dense_ffn_kernel(x_ref, wg_ref, wu_ref, wd_ref, y_ref, act_ref):
    y_ref[...] = _swiglu(x_ref[...], wg_ref, wu_ref, wd_ref, act_ref)


def _dense_ffn(x, w_gate, w_up, w_down):
    n = x.shape[0]
    tm = FFN_ROW_TILE
    row = lambda i: (i, 0)
    const = lambda i: (0, 0)
    return pl.pallas_call(
        _dense_ffn_kernel,
        grid=(n // tm,),
        in_specs=[
            pl.BlockSpec((tm, D_MODEL), row),
            pl.BlockSpec((D_MODEL, D_FF), const),
            pl.BlockSpec((D_MODEL, D_FF), const),
            pl.BlockSpec((D_FF, D_MODEL), const),
        ],
        out_specs=pl.BlockSpec((tm, D_MODEL), row),
        out_shape=jax.ShapeDtypeStruct((n, D_MODEL), F32),
        scratch_shapes=[pltpu.VMEM((tm, D_FF), BF16)],
        compiler_params=pltpu.CompilerParams(
            dimension_semantics=("arbitrary",), vmem_limit_bytes=_vmem_limit(58 << 20)),
        name="dense_swiglu",
    )(x, w_gate, w_up, w_down)


KEY_STRIDE = 1 << 16

ROW_SLABS = D_MODEL // LANES


def _rows_to_2d(ref3):
    return jnp.concatenate([ref3[:, s, :] for s in range(ROW_SLABS)], axis=1)


def _store_rows_3d(ref3, val):
    for s in range(ROW_SLABS):
        ref3[:, s, :] = val[:, s * LANES:(s + 1) * LANES]


ROW_COPY_UNROLL = 8


def _moe_ffn_kernel(te_ref, nt_ref, src_ref, src_next_ref, dst_ref, hn_ref, wg_ref, wu_ref, wd_ref,
                    out_ref, xbuf, ybuf, act_ref, sem_g, sem_s, *, n_tok):
    i = pl.program_id(0)
    nt = nt_ref[0]
    tmf = xbuf.shape[1]
    slot = i % 2
    dump0 = TOP_K * n_tok

    def gather(sref, slot_):
        def body(j, carry):
            for u in range(ROW_COPY_UNROLL):
                r = ROW_COPY_UNROLL * j + u
                pltpu.make_async_copy(hn_ref.at[sref[0, 0, r]], xbuf.at[slot_, r],
                                      sem_g.at[slot_]).start(priority=u % 2)
            return carry
        lax.fori_loop(0, tmf // ROW_COPY_UNROLL, body, 0)

    def wait_rows(buf, sem):
        pltpu.make_async_copy(hn_ref.at[pl.ds(0, tmf)], buf, sem).wait()

    @pl.when(i == 0)
    def _():
        ybuf[...] = jnp.zeros_like(ybuf)
        for s in range(2):
            fill = pltpu.make_async_copy(ybuf.at[s], out_ref.at[pl.ds(dump0 + s * tmf, tmf)], sem_s.at[s])
            fill.start()
            fill.wait()
        gather(src_ref, 0)

    @pl.when(i + 1 < nt)
    def _():
        gather(src_next_ref, 1 - slot)

    @pl.when(i < nt)
    def _():
        wait_rows(xbuf.at[slot], sem_g.at[slot])
        y = _swiglu(_rows_to_2d(xbuf.at[slot]).astype(BF16), wg_ref, wu_ref, wd_ref, act_ref)

        @pl.when(i >= 2)
        def _():
            wait_rows(ybuf.at[slot], sem_s.at[slot])

        _store_rows_3d(ybuf.at[slot], y)

        def body(j, carry):
            for u in range(ROW_COPY_UNROLL):
                r = ROW_COPY_UNROLL * j + u
                pltpu.make_async_copy(ybuf.at[slot, r], out_ref.at[dst_ref[0, 0, r]],
                                      sem_s.at[slot]).start(priority=u % 2)
            return carry
        lax.fori_loop(0, tmf // ROW_COPY_UNROLL, body, 0)

    @pl.when(i == pl.num_programs(0) - 1)
    def _():
        @pl.when(nt >= 2)
        def _():
            wait_rows(ybuf.at[nt % 2], sem_s.at[nt % 2])
        wait_rows(ybuf.at[(nt - 1) % 2], sem_s.at[(nt - 1) % 2])


def _moe_ffn(hn, w_gate, w_up, w_down, src_tiles, dst_tiles, tile_expert, n_tiles):
    n_tok = hn.shape[0]
    tmf = FFN_ROW_TILE
    t_total = src_tiles.shape[0]
    grid_spec = pltpu.PrefetchScalarGridSpec(
        num_scalar_prefetch=2,
        grid=(t_total,),
        in_specs=[
            pl.BlockSpec((1, 1, tmf), lambda i, te, nt: (i, 0, 0), memory_space=pltpu.SMEM),
            pl.BlockSpec((1, 1, tmf), lambda i, te, nt: (jnp.minimum(i + 1, t_total - 1), 0, 0),
                         memory_space=pltpu.SMEM),
            pl.BlockSpec((1, 1, tmf), lambda i, te, nt: (i, 0, 0), memory_space=pltpu.SMEM),
            pl.BlockSpec(memory_space=pl.ANY),
            pl.BlockSpec((None, D_MODEL, D_FF), lambda i, te, nt: (te[i], 0, 0)),
            pl.BlockSpec((None, D_MODEL, D_FF), lambda i, te, nt: (te[i], 0, 0)),
            pl.BlockSpec((None, D_FF, D_MODEL), lambda i, te, nt: (te[i], 0, 0)),
        ],
        out_specs=pl.BlockSpec(memory_space=pl.ANY),
        scratch_shapes=[
            pltpu.VMEM((2, tmf, ROW_SLABS, LANES), F32),
            pltpu.VMEM((2, tmf, ROW_SLABS, LANES), F32),
            pltpu.VMEM((tmf, D_FF), BF16),
            pltpu.SemaphoreType.DMA((2,)),
            pltpu.SemaphoreType.DMA((2,)),
        ],
    )
    return pl.pallas_call(
        functools.partial(_moe_ffn_kernel, n_tok=n_tok),
        grid_spec=grid_spec,
        out_shape=jax.ShapeDtypeStruct((TOP_K * n_tok + 2 * tmf, ROW_SLABS, LANES), F32),
        compiler_params=pltpu.CompilerParams(
            dimension_semantics=("arbitrary",), vmem_limit_bytes=_vmem_limit(58 << 20),
            has_side_effects=True),
        name="expert_swiglu",
    )(tile_expert, n_tiles, src_tiles, src_tiles, dst_tiles, hn, w_gate, w_up, w_down)


def _ple_tail(h2, p_tile, g_ple, wpg_ref, wpp_ref):
    gate = _sigmoid(jnp.dot(_rmsnorm(h2, g_ple).astype(BF16), wpg_ref[...], preferred_element_type=F32))
    return h2 + gate * jnp.dot(p_tile.astype(BF16), wpp_ref[...], preferred_element_type=F32)


def _combine_kernel(h1_ref, rf_ref, y0_ref, y1_ref, p_ref, gple_ref, wpg_ref, wpp_ref, gfin_ref,
                    out_ref, *, final_norm):
    rf = rf_ref[...]
    h2 = h1_ref[...] + rf[:, 0:1] * _rows_to_2d(y0_ref) + rf[:, 1:2] * _rows_to_2d(y1_ref)
    h3 = _ple_tail(h2, p_ref[...], gple_ref[...], wpg_ref, wpp_ref)
    out_ref[...] = _rmsnorm(h3, gfin_ref[...]) if final_norm else h3


def _combine(h1, rf, ys, p, g_ple, w_pg, w_pp, g_final, final_norm):
    n = h1.shape[0]
    tm = ROW_TILE
    row = lambda i: (i, 0)
    const = lambda i: (0, 0)
    return pl.pallas_call(
        functools.partial(_combine_kernel, final_norm=final_norm),
        grid=(n // tm,),
        in_specs=[
            pl.BlockSpec((tm, D_MODEL), row),
            pl.BlockSpec((tm, LANES), row),
            pl.BlockSpec((tm, ROW_SLABS, LANES), lambda i: (i, 0, 0)),
            pl.BlockSpec((tm, ROW_SLABS, LANES), lambda i: (n // tm + i, 0, 0)),
            pl.BlockSpec((tm, D_PLE), row),
            pl.BlockSpec((1, D_MODEL), const),
            pl.BlockSpec((D_MODEL, D_MODEL), const),
            pl.BlockSpec((D_PLE, D_MODEL), const),
            pl.BlockSpec((1, D_MODEL), const),
        ],
        out_specs=pl.BlockSpec((tm, D_MODEL), row),
        out_shape=jax.ShapeDtypeStruct((n, D_MODEL), F32),
        compiler_params=pltpu.CompilerParams(
            dimension_semantics=("arbitrary",), vmem_limit_bytes=_vmem_limit(40 << 20)),
        name="expert_combine_tail",
    )(h1, rf, ys, ys, p, g_ple, w_pg, w_pp, g_final)


def _dense_tail_kernel(h1_ref, y_ref, p_ref, gple_ref, wpg_ref, wpp_ref, gfin_ref, out_ref, *,
                       final_norm):
    h2 = h1_ref[...] + y_ref[...]
    h3 = _ple_tail(h2, p_ref[...], gple_ref[...], wpg_ref, wpp_ref)
    out_ref[...] = _rmsnorm(h3, gfin_ref[...]) if final_norm else h3


def _dense_tail(h1, y, p, g_ple, w_pg, w_pp, g_final, final_norm):
    n = h1.shape[0]
    tm = ROW_TILE
    row = lambda i: (i, 0)
    const = lambda i: (0, 0)
    return pl.pallas_call(
        functools.partial(_dense_tail_kernel, final_norm=final_norm),
        grid=(n // tm,),
        in_specs=[
            pl.BlockSpec((tm, D_MODEL), row),
            pl.BlockSpec((tm, D_MODEL), row),
            pl.BlockSpec((tm, D_PLE), row),
            pl.BlockSpec((1, D_MODEL), const),
            pl.BlockSpec((D_MODEL, D_MODEL), const),
            pl.BlockSpec((D_PLE, D_MODEL), const),
            pl.BlockSpec((1, D_MODEL), const),
        ],
        out_specs=pl.BlockSpec((tm, D_MODEL), row),
        out_shape=jax.ShapeDtypeStruct((n, D_MODEL), F32),
        compiler_params=pltpu.CompilerParams(
            dimension_semantics=("arbitrary",), vmem_limit_bytes=_vmem_limit(32 << 20)),
        name="dense_ple_tail",
    )(h1, y, p, g_ple, w_pg, w_pp, g_final)


def _block_diag(w):
    nb, db, _ = w.shape
    eye = jnp.eye(nb, dtype=w.dtype)
    return (eye[:, None, :, None] * w[:, :, None, :]).reshape(nb * db, nb * db)


def _mixer_layer(h, g_mix, w_in, b_in, w_conv_qk, b_conv_qk, g_mh, w_conv_r, b_conv_r,
                 w_ra, b_ra, w_ri, b_ri, lam, g_r, batch, seq):
    q_end, k_end, v_end, o_end = D_M, 2 * D_M, 3 * D_M, 4 * D_M
    i_end = o_end + NH_M
    f_end = i_end + NH_M
    xr_end = f_end + D_R
    w_main = jnp.concatenate([w_in[:, :o_end], w_in[:, f_end:]], axis=1).astype(BF16)
    b_main = jnp.concatenate([b_in[:o_end], b_in[f_end:]])[None, :]
    w_gt = w_in[:, o_end:f_end].T.astype(BF16)
    b_gt = b_in[o_end:f_end][:, None]
    del q_end, k_end, v_end, xr_end
    wbd = jnp.concatenate([_block_diag(w_ra), _block_diag(w_ri)], axis=1).astype(BF16)
    bbd = jnp.concatenate([b_ra, b_ri])[None, :]
    ones_bd = _block_diag(jnp.ones((NB_R, DB_R, DB_R), F32)).astype(BF16)
    q, k, v, og, a, u, gy, gt = _norm_inproj(
        h, g_mix[None, :], w_main, b_main, w_gt, b_gt, w_conv_qk, b_conv_qk[None, :],
        w_conv_r, b_conv_r[None, :], wbd, bbd, lam[None, :], seq)
    return _mixer(q, k, v, og, a, u, gy, gt, g_mh[None, :], g_r[None, :], ones_bd, batch, seq)


def _expert_tiles(top_idx, counts_f, n_tok):
    tmf = FFN_ROW_TILE
    counts = counts_f.astype(jnp.int32)
    padded = (counts + tmf - 1) // tmf * tmf
    n_tiles = (jnp.sum(padded) // tmf).astype(jnp.int32).reshape(1)
    a_ids = jnp.arange(TOP_K * n_tok, dtype=jnp.int32)
    real = top_idx.reshape(-1) * KEY_STRIDE + a_ids
    pad_j = jnp.arange(tmf, dtype=jnp.int32)[None, :]
    experts = jnp.arange(N_EXPERTS, dtype=jnp.int32)[:, None]
    pad_keys = jnp.where(pad_j < (padded - counts)[:, None],
                         experts * KEY_STRIDE + TOP_K * n_tok + pad_j, N_EXPERTS * KEY_STRIDE)
    keys = jnp.sort(jnp.concatenate([real, pad_keys.reshape(-1)]))
    t_total = (TOP_K * n_tok) // tmf + N_EXPERTS
    a_sorted = keys & (KEY_STRIDE - 1)
    valid = a_sorted < TOP_K * n_tok
    row = jnp.arange(t_total * tmf, dtype=jnp.int32)
    dump = TOP_K * n_tok + (row // tmf % 2) * tmf + row % tmf
    src = jnp.where(valid, a_sorted >> 1, 0).reshape(t_total, 1, tmf)
    dst = jnp.where(valid, (a_sorted & 1) * n_tok + (a_sorted >> 1), dump).reshape(t_total, 1, tmf)
    tile_expert = jnp.minimum(keys[::tmf] // KEY_STRIDE, N_EXPERTS - 1).astype(jnp.int32)
    return src, dst, tile_expert, n_tiles


def kernel(x, p, g_mix, w_in, b_in, w_conv_qk, b_conv_qk, g_mh, w_conv_r, b_conv_r, w_ra, b_ra, w_ri, b_ri, lam, g_r, w_out, g_ffn, w_ff_gate, w_ff_up, w_ff_down, w_router, w_e_gate, w_e_up, w_e_down, g_ple, w_ple_gate, w_ple_proj, g_final):
    batch, seq, _ = x.shape
    n = batch * seq
    depth = g_mix.shape[0]
    h = x.reshape(n, D_MODEL)
    tmf = FFN_ROW_TILE
    for i in range(depth):
        last = i == depth - 1
        hcat = _mixer_layer(h, g_mix[i], w_in[i], b_in[i], w_conv_qk[i], b_conv_qk[i], g_mh[i],
                            w_conv_r[i], b_conv_r[i], w_ra[i], b_ra[i], w_ri[i], b_ri[i], lam[i],
                            g_r[i], batch, seq)
        p_i = p[i].reshape(n, D_PLE)
        w_pg = w_ple_gate[i].astype(BF16)
        w_pp = w_ple_proj[i].astype(BF16)
        j = i // 2
        if i % 2 == 0:
            h1, hn = _outproj_norm(h, hcat, w_out[i].astype(BF16), g_ffn[i][None, :], BF16)
            y = _dense_ffn(hn, w_ff_gate[j].astype(BF16), w_ff_up[j].astype(BF16),
                           w_ff_down[j].astype(BF16))
            h = _dense_tail(h1, y, p_i, g_ple[i][None, :], w_pg, w_pp, g_final[None, :], last)
        else:
            w_r = jnp.zeros((D_MODEL, LANES), F32).at[:, :N_EXPERTS].set(w_router[j])
            h1, hn, ri, rf, cnt = _outproj_router(h, hcat, w_out[i].astype(BF16), g_ffn[i][None, :], w_r)
            src, dst, tile_expert, n_tiles = _expert_tiles(ri[:, :TOP_K], cnt[0, :N_EXPERTS], n)
            ys = _moe_ffn(hn, w_e_gate[j].astype(BF16), w_e_up[j].astype(BF16), w_e_down[j].astype(BF16),
                          src, dst, tile_expert, n_tiles)
            h = _combine(h1, rf, ys, p_i, g_ple[i][None, :], w_pg, w_pp, g_final[None, :], last)
    return h.reshape(batch, seq, D_MODEL)
```

```python
import functools

import jax
import jax.numpy as jnp
from jax import lax
from jax.experimental import pallas as pl
from jax.experimental.pallas import tpu as pltpu

F32 = jnp.float32
BF16 = jnp.bfloat16

D_MODEL = 1024
NH_M = 4
DH_M = 128
D_M = NH_M * DH_M
D_R = D_MODEL - D_M
NB_R = 8
DB_R = D_R // NB_R
CONV_W = 4
LRU_C = 8.0
D_FF = 2816
N_EXPERTS = 8
TOP_K = 2
D_PLE = 256
EPS = 1e-6

LANES = 128
SUBLANES = 8
V7X_VMEM_BYTES = 64 * 1024 * 1024

ROW_TILE = 512
MIX_CHUNK = 128
FFN_ROW_TILE = 512
FFN_COL_CHUNK = 256
INPROJ_COL_CHUNK = 256
INPROJ_ROW_TILE = 512


def _vmem_limit(nbytes):
    return int(min(nbytes, V7X_VMEM_BYTES - 6 * 1024 * 1024))


def _rmsnorm(x, g):
    return x * lax.rsqrt(jnp.mean(x * x, axis=-1, keepdims=True) + EPS) * g


def _sigmoid(x):
    return 1.0 / (1.0 + jnp.exp(-x))


def _silu(x):
    return x * _sigmoid(x)


def _causal_conv(x, prev, w, b, stage_ref):
    t, c = x.shape
    outs = []
    for l in range(c // LANES):
        cols = slice(l * LANES, (l + 1) * LANES)
        stage_ref[l, 0:SUBLANES, :] = prev[:, cols]
        stage_ref[l, SUBLANES:, :] = x[:, cols]
        acc = x[:, cols] * w[CONV_W - 1:CONV_W, cols] + b[:, cols]
        for j in range(CONV_W - 1):
            lo = SUBLANES - (CONV_W - 1) + j
            acc = acc + stage_ref[l, lo:lo + t, :] * w[j:j + 1, cols]
        outs.append(acc)
    return jnp.concatenate(outs, axis=1)


def _inproj_kernel(x_ref, g_ref, w_ref, b_ref, wg_ref, bg_ref, wcqk_ref, bcqk_ref, wcr_ref, bcr_ref,
                   wbd_ref, bbd_ref, lam_ref,
                   q_ref, k_ref, v_ref, og_ref, a_ref, u_ref, gy_ref, gt_ref,
                   pqk_ref, pxr_ref, xr_ref, cstage_ref, *, tiles_per_seq):
    tm = x_ref.shape[0]

    @pl.when(pl.program_id(0) % tiles_per_seq == 0)
    def _():
        pqk_ref[...] = jnp.zeros_like(pqk_ref)
        pxr_ref[...] = jnp.zeros_like(pxr_ref)

    xn = _rmsnorm(x_ref[...], g_ref[...]).astype(BF16)
    gt_ref[...] = lax.dot_general(wg_ref[...], xn, (((1,), (1,)), ((), ())),
                                  preferred_element_type=F32) + bg_ref[...]
    cw = INPROJ_COL_CHUNK

    n_stage = [0]

    def next_stage():
        n_stage[0] += 1
        return cstage_ref.at[n_stage[0] % 2]

    def post_qk(z, c0):
        cols = slice(c0, c0 + cw)
        y = _silu(_causal_conv(z, pqk_ref[:, cols], wcqk_ref[:, cols], bcqk_ref[:, cols], next_stage()))
        pqk_ref[:, cols] = z[tm - SUBLANES:, :]
        if c0 < D_M:
            q_ref[:, cols] = (y * (DH_M ** -0.5)).astype(BF16)
        else:
            k_ref[:, c0 - D_M:c0 - D_M + cw] = y.astype(BF16)

    def post_v(z, c0):
        v_ref[:, c0:c0 + cw] = z.astype(BF16)

    def post_og(z, c0):
        og_ref[:, c0:c0 + cw] = _sigmoid(z).astype(BF16)

    def post_xr(z, c0):
        cols = slice(c0, c0 + cw)
        xr_ref[:, cols] = _causal_conv(z, pxr_ref[:, cols], wcr_ref[:, cols], bcr_ref[:, cols],
                                       next_stage())
        pxr_ref[:, cols] = z[tm - SUBLANES:, :]

    def post_gy(z, c0):
        gy_ref[:, c0:c0 + cw] = (
            0.5 * z * (1.0 + jnp.tanh(0.7978845608028654 * (z + 0.044715 * z * z * z)))).astype(BF16)

    nlam = -lam_ref[...]
    softplus = jnp.maximum(nlam, 0.0) + jnp.log(1.0 + jnp.exp(-jnp.abs(nlam)))

    def project(col):
        return lambda: (jnp.dot(xn, w_ref[:, col:col + cw], preferred_element_type=F32)
                        + b_ref[:, col:col + cw])

    def gate_matmuls(c0):
        def run():
            xr_bf = xr_ref[...].astype(BF16)
            return tuple(jnp.dot(xr_bf, wbd_ref[:, g0 + c0:g0 + c0 + cw], preferred_element_type=F32)
                         + bbd_ref[:, g0 + c0:g0 + c0 + cw] for g0 in (0, D_R))
        return run

    def post_gates(z, c0):
        cols = slice(c0, c0 + cw)
        log_a = (-LRU_C) * _sigmoid(z[0]) * softplus[:, cols]
        a_ref[:, cols] = jnp.exp(log_a)
        u_ref[:, cols] = jnp.sqrt(1.0 - jnp.exp(2.0 * log_a)) * (_sigmoid(z[1]) * xr_ref[:, cols])

    stages = []
    for base, width, post in ((4 * D_M, D_R, post_xr), (4 * D_M + D_R, D_R, post_gy)):
        stages += [(project(base + c0), c0, post) for c0 in range(0, width, cw)]
    stages += [(gate_matmuls(c0), c0, post_gates) for c0 in range(0, D_R, cw)]
    for base, width, post in ((0, 2 * D_M, post_qk), (3 * D_M, D_M, post_og), (2 * D_M, D_M, post_v)):
        stages += [(project(base + c0), c0, post) for c0 in range(0, width, cw)]

    z_next = stages[0][0]()
    for idx, (_, c0, post) in enumerate(stages):
        z = z_next
        if idx + 1 < len(stages):
            z_next = stages[idx + 1][0]()
        post(z, c0)


def _norm_inproj(h, g, w_main, b_main, w_gt, b_gt, wcqk, bcqk, wcr, bcr, wbd, bbd, lam, seq):
    n = h.shape[0]
    tm = INPROJ_ROW_TILE
    wcols = w_main.shape[1]
    row = lambda i: (i, 0)
    const = lambda i: (0, 0)
    return pl.pallas_call(
        functools.partial(_inproj_kernel, tiles_per_seq=seq // tm),
        grid=(n // tm,),
        in_specs=[
            pl.BlockSpec((tm, D_MODEL), row),
            pl.BlockSpec((1, D_MODEL), const),
            pl.BlockSpec((D_MODEL, wcols), const),
            pl.BlockSpec((1, wcols), const),
            pl.BlockSpec((2 * NH_M, D_MODEL), const),
            pl.BlockSpec((2 * NH_M, 1), const),
            pl.BlockSpec((CONV_W, 2 * D_M), const),
            pl.BlockSpec((1, 2 * D_M), const),
            pl.BlockSpec((CONV_W, D_R), const),
            pl.BlockSpec((1, D_R), const),
            pl.BlockSpec((D_R, 2 * D_R), const),
            pl.BlockSpec((1, 2 * D_R), const),
            pl.BlockSpec((1, D_R), const),
        ],
        out_specs=[
            pl.BlockSpec((tm, D_M), row),
            pl.BlockSpec((tm, D_M), row),
            pl.BlockSpec((tm, D_M), row),
            pl.BlockSpec((tm, D_M), row),
            pl.BlockSpec((tm, D_R), row),
            pl.BlockSpec((tm, D_R), row),
            pl.BlockSpec((tm, D_R), row),
            pl.BlockSpec((2 * NH_M, tm), lambda i: (0, i)),
        ],
        out_shape=[
            jax.ShapeDtypeStruct((n, D_M), BF16),
            jax.ShapeDtypeStruct((n, D_M), BF16),
            jax.ShapeDtypeStruct((n, D_M), BF16),
            jax.ShapeDtypeStruct((n, D_M), BF16),
            jax.ShapeDtypeStruct((n, D_R), F32),
            jax.ShapeDtypeStruct((n, D_R), F32),
            jax.ShapeDtypeStruct((n, D_R), BF16),
            jax.ShapeDtypeStruct((2 * NH_M, n), F32),
        ],
        scratch_shapes=[pltpu.VMEM((SUBLANES, 2 * D_M), F32), pltpu.VMEM((SUBLANES, D_R), F32),
                        pltpu.VMEM((tm, D_R), F32),
                        pltpu.VMEM((2, INPROJ_COL_CHUNK // LANES, tm + SUBLANES, LANES), F32)],
        compiler_params=pltpu.CompilerParams(
            dimension_semantics=("arbitrary",), vmem_limit_bytes=_vmem_limit(48 << 20)),
        name="norm_inproj",
    )(h, g, w_main, b_main, w_gt, b_gt, wcqk, bcqk, wcr, bcr, wbd, bbd, lam)


def _mixer_kernel(q_ref, k_ref, v_ref, og_ref, a_ref, u_ref, gy_ref, gt_ref, gmh_ref, gr_ref, ones_ref,
                  *rest, n_cast):
    cast_in, (out_ref,), cast_out = rest[:n_cast], rest[n_cast:n_cast + 1], rest[n_cast + 1:2 * n_cast + 1]
    s_ref, m_ref, hr_ref = rest[2 * n_cast + 1:]
    tm = q_ref.shape[0]
    for src, dst in zip(cast_in, cast_out):
        dst[...] = src[...].astype(dst.dtype)

    @pl.when(pl.program_id(1) == 0)
    def _():
        s_ref[...] = jnp.zeros_like(s_ref)
        m_ref[...] = jnp.zeros_like(m_ref)
        hr_ref[...] = jnp.zeros_like(hr_ref)

    gt = gt_ref[...]
    logf = jnp.minimum(gt, 0.0) - jnp.log(1.0 + jnp.exp(-jnp.abs(gt)))
    L = MIX_CHUNK
    ri = lax.broadcasted_iota(jnp.int32, (L, L), 0)
    ci = lax.broadcasted_iota(jnp.int32, (L, L), 1)
    tril = ci <= ri
    upper = (ri <= ci).astype(F32)
    ones_blk = jnp.ones((L, DH_M), BF16)
    neg_inf = -jnp.inf

    b_all = jnp.concatenate(
        [jnp.dot(logf[:, c * L:(c + 1) * L], upper, preferred_element_type=F32,
                 precision=lax.Precision.HIGHEST) for c in range(tm // L)], axis=1)

    for h in range(NH_M):
        s_state = s_ref[h]
        m_prev = m_ref[h]
        for c in range(tm // L):
            r0 = c * L
            q = q_ref[r0:r0 + L, h * DH_M:(h + 1) * DH_M]
            kt32 = k_ref[r0:r0 + L, h * DH_M:(h + 1) * DH_M].astype(F32).T
            v_aug = jnp.concatenate([v_ref[r0:r0 + L, h * DH_M:(h + 1) * DH_M], ones_blk], axis=1)
            ig_row = gt[h:h + 1, r0:r0 + L]
            lf_row = logf[NH_M + h:NH_M + h + 1, r0:r0 + L]
            b_row = b_all[NH_M + h:NH_M + h + 1, r0:r0 + L]
            c_row = ig_row - b_row
            g_col = jnp.maximum(jnp.max(jnp.where(tril, c_row, neg_inf), axis=-1, keepdims=True), m_prev)
            b_col = jnp.sum(jnp.where(tril, lf_row, 0.0), axis=-1, keepdims=True)
            g_last = g_col[L - 1:L, :]
            b_last = b_col[L - 1:L, :]
            dmat = jnp.exp(jnp.where(tril, c_row - g_col, neg_inf))
            s = jnp.dot(q, kt32.astype(BF16), preferred_element_type=F32) * dmat
            inter = jnp.exp(m_prev - g_col)
            lhs = jnp.concatenate([s.astype(BF16), (q.astype(F32) * inter).astype(BF16)], axis=1)
            rhs = jnp.concatenate([v_aug, s_state.astype(BF16)], axis=0)
            out = jnp.dot(lhs, rhs, preferred_element_type=F32)
            num = out[:, :DH_M]
            den = out[:, DH_M:]
            hm = num / jnp.maximum(jnp.abs(den), jnp.exp(-(b_col + g_col)))
            hm = hm * og_ref[r0:r0 + L, h * DH_M:(h + 1) * DH_M].astype(F32)
            hm = hm * lax.rsqrt(jnp.mean(hm * hm, axis=-1, keepdims=True) + EPS)
            out_ref[r0:r0 + L, h * DH_M:(h + 1) * DH_M] = (
                hm * gmh_ref[:, h * DH_M:(h + 1) * DH_M]).astype(out_ref.dtype)
            wk_row = jnp.exp(c_row - g_last)
            decay = jnp.exp(m_prev - g_last)
            s_state = decay * s_state + jnp.dot((kt32 * wk_row).astype(BF16), v_aug,
                                                preferred_element_type=F32)
            m_prev = b_last + g_last
        s_ref[h] = s_state
        m_ref[h] = m_prev

    groups = tm // SUBLANES
    a = a_ref[...].reshape(groups, SUBLANES, D_R)
    u = u_ref[...].reshape(groups, SUBLANES, D_R)
    row8 = lax.broadcasted_iota(jnp.int32, (groups, SUBLANES, D_R), 1)
    for sft in (1, 2, 4):
        keep = row8 >= sft
        a_sh = pltpu.roll(a, sft, axis=1)
        u_sh = pltpu.roll(u, sft, axis=1)
        u = jnp.where(keep, a * u_sh + u, u)
        a = jnp.where(keep, a * a_sh, a)
    hprev = hr_ref[...]
    rows = []
    for gi in range(groups):
        blk = u[gi] + a[gi] * hprev
        rows.append(blk)
        hprev = blk[SUBLANES - 1:SUBLANES, :]
    hr_ref[...] = hprev
    hr = jnp.concatenate(rows, axis=0) * gy_ref[...].astype(F32)
    sq = hr * hr
    sq_hi = sq.astype(BF16)
    sq_lo = (sq - sq_hi.astype(F32)).astype(BF16)
    gsum = (jnp.dot(sq_hi, ones_ref[...], preferred_element_type=F32)
            + jnp.dot(sq_lo, ones_ref[...], preferred_element_type=F32))
    hr = hr * lax.rsqrt(gsum * (1.0 / DB_R) + EPS) * gr_ref[...]
    out_ref[:, D_M:] = hr.astype(out_ref.dtype)


def _cast_slices(w, steps):
    rows, cols = w.shape
    used = max(d for d in range(1, steps + 1)
               if steps % d == 0 and rows % d == 0 and (rows // d) % (2 * SUBLANES) == 0)
    per = steps // used
    return (rows // used, cols), per


def _mixer(q, k, v, og, a, u, gy, gt, gmh, gr, ones_bd, batch, seq, to_bf16=()):
    tm = ROW_TILE
    spb = seq // tm
    row = lambda b, s: (b * spb + s, 0)
    const = lambda b, s: (0, 0)
    n = batch * seq
    cast_specs = []
    for w in to_bf16:
        blk, per = _cast_slices(w, batch * spb)
        cast_specs.append(pl.BlockSpec(blk, functools.partial(
            lambda b, s, per: ((b * spb + s) // per, 0), per=per)))
    outs = pl.pallas_call(
        functools.partial(_mixer_kernel, n_cast=len(to_bf16)),
        grid=(batch, spb),
        in_specs=[
            pl.BlockSpec((tm, D_M), row),
            pl.BlockSpec((tm, D_M), row),
            pl.BlockSpec((tm, D_M), row),
            pl.BlockSpec((tm, D_M), row),
            pl.BlockSpec((tm, D_R), row),
            pl.BlockSpec((tm, D_R), row),
            pl.BlockSpec((tm, D_R), row),
            pl.BlockSpec((2 * NH_M, tm), lambda b, s: (0, b * spb + s)),
            pl.BlockSpec((1, D_M), const),
            pl.BlockSpec((1, D_R), const),
            pl.BlockSpec((D_R, D_R), const),
        ] + cast_specs,
        out_specs=[pl.BlockSpec((tm, D_MODEL), row)] + cast_specs,
        out_shape=[jax.ShapeDtypeStruct((n, D_MODEL), BF16)]
        + [jax.ShapeDtypeStruct(w.shape, BF16) for w in to_bf16],
        scratch_shapes=[
            pltpu.VMEM((NH_M, DH_M, 2 * DH_M), F32),
            pltpu.VMEM((NH_M, 1, 1), F32),
            pltpu.VMEM((1, D_R), F32),
        ],
        compiler_params=pltpu.CompilerParams(
            dimension_semantics=("arbitrary", "arbitrary"), vmem_limit_bytes=_vmem_limit(58 << 20)),
        name="seq_mixer",
    )(q, k, v, og, a, u, gy, gt, gmh, gr, ones_bd, *to_bf16)
    return outs[0], outs[1:]


def _outproj_kernel(h_ref, hc_ref, w_ref, g_ref, h1_ref, hn_ref):
    h1 = h_ref[...] + jnp.dot(hc_ref[...], w_ref[...], preferred_element_type=F32)
    h1_ref[...] = h1
    hn_ref[...] = _rmsnorm(h1, g_ref[...]).astype(hn_ref.dtype)


def _outproj_norm(h, hcat, w_out, g, hn_dtype):
    n = h.shape[0]
    tm = ROW_TILE
    row = lambda i: (i, 0)
    const = lambda i: (0, 0)
    return pl.pallas_call(
        _outproj_kernel,
        grid=(n // tm,),
        in_specs=[
            pl.BlockSpec((tm, D_MODEL), row),
            pl.BlockSpec((tm, D_MODEL), row),
            pl.BlockSpec((D_MODEL, D_MODEL), const),
            pl.BlockSpec((1, D_MODEL), const),
        ],
        out_specs=[pl.BlockSpec((tm, D_MODEL), row), pl.BlockSpec((tm, D_MODEL), row)],
        out_shape=[jax.ShapeDtypeStruct((n, D_MODEL), F32), jax.ShapeDtypeStruct((n, D_MODEL), hn_dtype)],
        compiler_params=pltpu.CompilerParams(
            dimension_semantics=("arbitrary",), vmem_limit_bytes=_vmem_limit(32 << 20)),
        name="outproj_norm",
    )(h, hcat, w_out, g)


def _outproj_router_kernel(h_ref, hc_ref, w_ref, g_ref, wr_ref, h1_ref, hn_ref, ri_ref, rf_ref, cnt_ref,
                           carry_ref):
    tm = h_ref.shape[0]

    @pl.when(pl.program_id(0) == 0)
    def _():
        carry_ref[...] = jnp.zeros_like(carry_ref)

    h1 = h_ref[...] + jnp.dot(hc_ref[...], w_ref[...], preferred_element_type=F32)
    h1_ref[...] = h1
    hn = _rmsnorm(h1, g_ref[...])
    _store_rows_3d(hn_ref, hn)
    lane_i = lax.broadcasted_iota(jnp.int32, (tm, LANES), 1)
    lane = lane_i.astype(F32)
    hn_hi = hn.astype(BF16)
    hn_lo = (hn - hn_hi.astype(F32)).astype(BF16)
    wr = wr_ref[...]
    wr_hi = wr.astype(BF16)
    wr_lo = (wr - wr_hi.astype(F32)).astype(BF16)
    logits = (jnp.dot(hn_hi, wr_hi, preferred_element_type=F32)
              + (jnp.dot(hn_hi, wr_lo, preferred_element_type=F32)
                 + jnp.dot(hn_lo, wr_hi, preferred_element_type=F32)))
    logits = jnp.where(lane_i < N_EXPERTS, logits, -jnp.inf)
    v1 = jnp.max(logits, axis=-1, keepdims=True)
    i1 = jnp.min(jnp.where(logits == v1, lane, float(LANES)), axis=-1, keepdims=True)
    m1 = lane == i1
    rest = jnp.where(m1, -jnp.inf, logits)
    v2 = jnp.max(rest, axis=-1, keepdims=True)
    i2 = jnp.min(jnp.where(rest == v2, lane, float(LANES)), axis=-1, keepdims=True)
    m2 = lane == i2
    e2 = jnp.exp(v2 - v1)
    g1 = 1.0 / (1.0 + e2)
    g2 = e2 / (1.0 + e2)
    carry = carry_ref[...] + jnp.sum(jnp.where(m1 | m2, 1.0, 0.0), axis=0, keepdims=True)
    carry_ref[...] = carry
    ri_ref[...] = jnp.where(lane_i == 0, i1.astype(jnp.int32), i2.astype(jnp.int32))
    rf_ref[...] = jnp.where(lane_i == 0, g1, g2)
    cnt_ref[...] = jnp.broadcast_to(carry, cnt_ref.shape)


def _outproj_router(h, hcat, w_out, g, w_router_pad):
    n = h.shape[0]
    tm = ROW_TILE
    row = lambda i: (i, 0)
    const = lambda i: (0, 0)
    return pl.pallas_call(
        _outproj_router_kernel,
        grid=(n // tm,),
        in_specs=[
            pl.BlockSpec((tm, D_MODEL), row),
            pl.BlockSpec((tm, D_MODEL), row),
            pl.BlockSpec((D_MODEL, D_MODEL), const),
            pl.BlockSpec((1, D_MODEL), const),
            pl.BlockSpec((D_MODEL, LANES), const),
        ],
        out_specs=[pl.BlockSpec((tm, D_MODEL), row),
                   pl.BlockSpec((tm, ROW_SLABS, LANES), lambda i: (i, 0, 0)),
                   pl.BlockSpec((tm, LANES), row), pl.BlockSpec((tm, LANES), row),
                   pl.BlockSpec((SUBLANES, LANES), const)],
        out_shape=[jax.ShapeDtypeStruct((n, D_MODEL), F32),
                   jax.ShapeDtypeStruct((n, ROW_SLABS, LANES), F32),
                   jax.ShapeDtypeStruct((n, LANES), jnp.int32), jax.ShapeDtypeStruct((n, LANES), F32),
                   jax.ShapeDtypeStruct((SUBLANES, LANES), F32)],
        scratch_shapes=[pltpu.VMEM((1, LANES), F32)],
        compiler_params=pltpu.CompilerParams(
            dimension_semantics=("arbitrary",), vmem_limit_bytes=_vmem_limit(40 << 20)),
        name="outproj_router",
    )(h, hcat, w_out, g, w_router_pad)


def _swiglu(x, wg_ref, wu_ref, wd_ref, act_ref):
    for f0 in range(0, D_FF, FFN_COL_CHUNK):
        g = jnp.dot(x, wg_ref[:, f0:f0 + FFN_COL_CHUNK], preferred_element_type=F32)
        u = jnp.dot(x, wu_ref[:, f0:f0 + FFN_COL_CHUNK], preferred_element_type=F32)
        act_ref[:, f0:f0 + FFN_COL_CHUNK] = (_silu(g) * u).astype(BF16)
    return jnp.dot(act_ref[...], wd_ref[...], preferred_element_type=F32)


def _dense_ffn_kernel(x_ref, wg_ref, wu_ref, wd_ref, y_ref, act_ref):
    y_ref[...] = _swiglu(x_ref[...], wg_ref, wu_ref, wd_ref, act_ref)


def _dense_ffn(x, w_gate, w_up, w_down):
    n = x.shape[0]
    tm = FFN_ROW_TILE
    row = lambda i: (i, 0)
    const = lambda i: (0, 0)
    return pl.pallas_call(
        _dense_ffn_kernel,
        grid=(n // tm,),
        in_specs=[
            pl.BlockSpec((tm, D_MODEL), row),
            pl.BlockSpec((D_MODEL, D_FF), const),
            pl.BlockSpec((D_MODEL, D_FF), const),
            pl.BlockSpec((D_FF, D_MODEL), const),
        ],
        out_specs=pl.BlockSpec((tm, D_MODEL), row),
        out_shape=jax.ShapeDtypeStruct((n, D_MODEL), F32),
        scratch_shapes=[pltpu.VMEM((tm, D_FF), BF16)],
        compiler_params=pltpu.CompilerParams(
            dimension_semantics=("arbitrary",), vmem_limit_bytes=_vmem_limit(58 << 20)),
        name="dense_swiglu",
    )(x, w_gate, w_up, w_down)


KEY_STRIDE = 1 << 16

ROW_SLABS = D_MODEL // LANES


def _rows_to_2d(ref3):
    return jnp.concatenate([ref3[:, s, :] for s in range(ROW_SLABS)], axis=1)


def _store_rows_3d(ref3, val):
    for s in range(ROW_SLABS):
        ref3[:, s, :] = val[:, s * LANES:(s + 1) * LANES]


ROW_COPY_UNROLL = 8


def _moe_ffn_kernel(te_ref, nt_ref, src_ref, src_next_ref, dst_ref, hn_ref, wg_ref, wu_ref, wd_ref,
                    out_ref, xbuf, ybuf, act_ref, sem_g, sem_s, *, n_tok):
    i = pl.program_id(0)
    nt = nt_ref[0]
    tmf = xbuf.shape[1]
    slot = i % 2
    dump0 = TOP_K * n_tok

    def gather(sref, slot_):
        def body(j, carry):
            for u in range(ROW_COPY_UNROLL):
                r = ROW_COPY_UNROLL * j + u
                pltpu.make_async_copy(hn_ref.at[sref[0, 0, r]], xbuf.at[slot_, r],
                                      sem_g.at[slot_]).start(priority=u % 2)
            return carry
        lax.fori_loop(0, tmf // ROW_COPY_UNROLL, body, 0)

    def wait_rows(buf, sem):
        pltpu.make_async_copy(hn_ref.at[pl.ds(0, tmf)], buf, sem).wait()

    @pl.when(i == 0)
    def _():
        ybuf[...] = jnp.zeros_like(ybuf)
        for s in range(2):
            fill = pltpu.make_async_copy(ybuf.at[s], out_ref.at[pl.ds(dump0 + s * tmf, tmf)], sem_s.at[s])
            fill.start()
            fill.wait()
        gather(src_ref, 0)

    @pl.when(i + 1 < nt)
    def _():
        gather(src_next_ref, 1 - slot)

    @pl.when(i < nt)
    def _():
        wait_rows(xbuf.at[slot], sem_g.at[slot])
        y = _swiglu(_rows_to_2d(xbuf.at[slot]).astype(BF16), wg_ref, wu_ref, wd_ref, act_ref)

        @pl.when(i >= 2)
        def _():
            wait_rows(ybuf.at[slot], sem_s.at[slot])

        _store_rows_3d(ybuf.at[slot], y)

        def body(j, carry):
            for u in range(ROW_COPY_UNROLL):
                r = ROW_COPY_UNROLL * j + u
                pltpu.make_async_copy(ybuf.at[slot, r], out_ref.at[dst_ref[0, 0, r]],
                                      sem_s.at[slot]).start(priority=u % 2)
            return carry
        lax.fori_loop(0, tmf // ROW_COPY_UNROLL, body, 0)

    @pl.when(i == pl.num_programs(0) - 1)
    def _():
        @pl.when(nt >= 2)
        def _():
            wait_rows(ybuf.at[nt % 2], sem_s.at[nt % 2])
        wait_rows(ybuf.at[(nt - 1) % 2], sem_s.at[(nt - 1) % 2])


def _moe_ffn(hn, w_gate, w_up, w_down, src_tiles, dst_tiles, tile_expert, n_tiles):
    n_tok = hn.shape[0]
    tmf = FFN_ROW_TILE
    t_total = src_tiles.shape[0]
    grid_spec = pltpu.PrefetchScalarGridSpec(
        num_scalar_prefetch=2,
        grid=(t_total,),
        in_specs=[
            pl.BlockSpec((1, 1, tmf), lambda i, te, nt: (i, 0, 0), memory_space=pltpu.SMEM),
            pl.BlockSpec((1, 1, tmf), lambda i, te, nt: (jnp.minimum(i + 1, t_total - 1), 0, 0),
                         memory_space=pltpu.SMEM),
            pl.BlockSpec((1, 1, tmf), lambda i, te, nt: (i, 0, 0), memory_space=pltpu.SMEM),
            pl.BlockSpec(memory_space=pl.ANY),
            pl.BlockSpec((None, D_MODEL, D_FF), lambda i, te, nt: (te[i], 0, 0)),
            pl.BlockSpec((None, D_MODEL, D_FF), lambda i, te, nt: (te[i], 0, 0)),
            pl.BlockSpec((None, D_FF, D_MODEL), lambda i, te, nt: (te[i], 0, 0)),
        ],
        out_specs=pl.BlockSpec(memory_space=pl.ANY),
        scratch_shapes=[
            pltpu.VMEM((2, tmf, ROW_SLABS, LANES), F32),
            pltpu.VMEM((2, tmf, ROW_SLABS, LANES), F32),
            pltpu.VMEM((tmf, D_FF), BF16),
            pltpu.SemaphoreType.DMA((2,)),
            pltpu.SemaphoreType.DMA((2,)),
        ],
    )
    return pl.pallas_call(
        functools.partial(_moe_ffn_kernel, n_tok=n_tok),
        grid_spec=grid_spec,
        out_shape=jax.ShapeDtypeStruct((TOP_K * n_tok + 2 * tmf, ROW_SLABS, LANES), F32),
        compiler_params=pltpu.CompilerParams(
            dimension_semantics=("arbitrary",), vmem_limit_bytes=_vmem_limit(58 << 20),
            has_side_effects=True),
        name="expert_swiglu",
    )(tile_expert, n_tiles, src_tiles, src_tiles, dst_tiles, hn, w_gate, w_up, w_down)


def _ple_tail(h2, p_tile, g_ple, wpg_ref, wpp_ref):
    gate = _sigmoid(jnp.dot(_rmsnorm(h2, g_ple).astype(BF16), wpg_ref[...], preferred_element_type=F32))
    return h2 + gate * jnp.dot(p_tile.astype(BF16), wpp_ref[...], preferred_element_type=F32)


def _combine_kernel(h1_ref, rf_ref, y0_ref, y1_ref, p_ref, gple_ref, wpg_ref, wpp_ref, gfin_ref,
                    out_ref, *, final_norm):
    rf = rf_ref[...]
    h2 = h1_ref[...] + rf[:, 0:1] * _rows_to_2d(y0_ref) + rf[:, 1:2] * _rows_to_2d(y1_ref)
    h3 = _ple_tail(h2, p_ref[...], gple_ref[...], wpg_ref, wpp_ref)
    out_ref[...] = _rmsnorm(h3, gfin_ref[...]) if final_norm else h3


def _combine(h1, rf, ys, p, g_ple, w_pg, w_pp, g_final, final_norm):
    n = h1.shape[0]
    tm = ROW_TILE
    row = lambda i: (i, 0)
    const = lambda i: (0, 0)
    return pl.pallas_call(
        functools.partial(_combine_kernel, final_norm=final_norm),
        grid=(n // tm,),
        in_specs=[
            pl.BlockSpec((tm, D_MODEL), row),
            pl.BlockSpec((tm, LANES), row),
            pl.BlockSpec((tm, ROW_SLABS, LANES), lambda i: (i, 0, 0)),
            pl.BlockSpec((tm, ROW_SLABS, LANES), lambda i: (n // tm + i, 0, 0)),
            pl.BlockSpec((tm, D_PLE), row),
            pl.BlockSpec((1, D_MODEL), const),
            pl.BlockSpec((D_MODEL, D_MODEL), const),
            pl.BlockSpec((D_PLE, D_MODEL), const),
            pl.BlockSpec((1, D_MODEL), const),
        ],
        out_specs=pl.BlockSpec((tm, D_MODEL), row),
        out_shape=jax.ShapeDtypeStruct((n, D_MODEL), F32),
        compiler_params=pltpu.CompilerParams(
            dimension_semantics=("arbitrary",), vmem_limit_bytes=_vmem_limit(40 << 20)),
        name="expert_combine_tail",
    )(h1, rf, ys, ys, p, g_ple, w_pg, w_pp, g_final)


def _dense_tail_kernel(h1_ref, y_ref, p_ref, gple_ref, wpg_ref, wpp_ref, gfin_ref, out_ref, *,
                       final_norm):
    h2 = h1_ref[...] + y_ref[...]
    h3 = _ple_tail(h2, p_ref[...], gple_ref[...], wpg_ref, wpp_ref)
    out_ref[...] = _rmsnorm(h3, gfin_ref[...]) if final_norm else h3


def _dense_tail(h1, y, p, g_ple, w_pg, w_pp, g_final, final_norm):
    n = h1.shape[0]
    tm = ROW_TILE
    row = lambda i: (i, 0)
    const = lambda i: (0, 0)
    return pl.pallas_call(
        functools.partial(_dense_tail_kernel, final_norm=final_norm),
        grid=(n // tm,),
        in_specs=[
            pl.BlockSpec((tm, D_MODEL), row),
            pl.BlockSpec((tm, D_MODEL), row),
            pl.BlockSpec((tm, D_PLE), row),
            pl.BlockSpec((1, D_MODEL), const),
            pl.BlockSpec((D_MODEL, D_MODEL), const),
            pl.BlockSpec((D_PLE, D_MODEL), const),
            pl.BlockSpec((1, D_MODEL), const),
        ],
        out_specs=pl.BlockSpec((tm, D_MODEL), row),
        out_shape=jax.ShapeDtypeStruct((n, D_MODEL), F32),
        compiler_params=pltpu.CompilerParams(
            dimension_semantics=("arbitrary",), vmem_limit_bytes=_vmem_limit(32 << 20)),
        name="dense_ple_tail",
    )(h1, y, p, g_ple, w_pg, w_pp, g_final)


def _block_diag(w):
    nb, db, _ = w.shape
    eye = jnp.eye(nb, dtype=w.dtype)
    return (eye[:, None, :, None] * w[:, :, None, :]).reshape(nb * db, nb * db)


def _mixer_layer(h, g_mix, w_in, b_in, w_conv_qk, b_conv_qk, g_mh, w_conv_r, b_conv_r,
                 w_ra, b_ra, w_ri, b_ri, lam, g_r, batch, seq, to_bf16):
    q_end, k_end, v_end, o_end = D_M, 2 * D_M, 3 * D_M, 4 * D_M
    i_end = o_end + NH_M
    f_end = i_end + NH_M
    xr_end = f_end + D_R
    w_main = jnp.concatenate([w_in[:, :o_end], w_in[:, f_end:]], axis=1).astype(BF16)
    b_main = jnp.concatenate([b_in[:o_end], b_in[f_end:]])[None, :]
    w_gt = w_in[:, o_end:f_end].T.astype(BF16)
    b_gt = b_in[o_end:f_end][:, None]
    del q_end, k_end, v_end, xr_end
    wbd = jnp.concatenate([_block_diag(w_ra), _block_diag(w_ri)], axis=1).astype(BF16)
    bbd = jnp.concatenate([b_ra, b_ri])[None, :]
    ones_bd = _block_diag(jnp.ones((NB_R, DB_R, DB_R), F32)).astype(BF16)
    q, k, v, og, a, u, gy, gt = _norm_inproj(
        h, g_mix[None, :], w_main, b_main, w_gt, b_gt, w_conv_qk, b_conv_qk[None, :],
        w_conv_r, b_conv_r[None, :], wbd, bbd, lam[None, :], seq)
    return _mixer(q, k, v, og, a, u, gy, gt, g_mh[None, :], g_r[None, :], ones_bd, batch, seq, to_bf16)


def _expert_tiles(top_idx, counts_f, n_tok):
    tmf = FFN_ROW_TILE
    counts = counts_f.astype(jnp.int32)
    padded = (counts + tmf - 1) // tmf * tmf
    n_tiles = (jnp.sum(padded) // tmf).astype(jnp.int32).reshape(1)
    a_ids = jnp.arange(TOP_K * n_tok, dtype=jnp.int32)
    real = top_idx.reshape(-1) * KEY_STRIDE + a_ids
    pad_j = jnp.arange(tmf, dtype=jnp.int32)[None, :]
    experts = jnp.arange(N_EXPERTS, dtype=jnp.int32)[:, None]
    pad_keys = jnp.where(pad_j < (padded - counts)[:, None],
                         experts * KEY_STRIDE + TOP_K * n_tok + pad_j, N_EXPERTS * KEY_STRIDE)
    keys = jnp.sort(jnp.concatenate([real, pad_keys.reshape(-1)]))
    t_total = (TOP_K * n_tok) // tmf + N_EXPERTS
    a_sorted = keys & (KEY_STRIDE - 1)
    valid = a_sorted < TOP_K * n_tok
    row = jnp.arange(t_total * tmf, dtype=jnp.int32)
    dump = TOP_K * n_tok + (row // tmf % 2) * tmf + row % tmf
    src = jnp.where(valid, a_sorted >> 1, 0).reshape(t_total, 1, tmf)
    dst = jnp.where(valid, (a_sorted & 1) * n_tok + (a_sorted >> 1), dump).reshape(t_total, 1, tmf)
    tile_expert = jnp.minimum(keys[::tmf] // KEY_STRIDE, N_EXPERTS - 1).astype(jnp.int32)
    return src, dst, tile_expert, n_tiles


def kernel(x, p, g_mix, w_in, b_in, w_conv_qk, b_conv_qk, g_mh, w_conv_r, b_conv_r, w_ra, b_ra, w_ri, b_ri, lam, g_r, w_out, g_ffn, w_ff_gate, w_ff_up, w_ff_down, w_router, w_e_gate, w_e_up, w_e_down, g_ple, w_ple_gate, w_ple_proj, g_final):
    batch, seq, _ = x.shape
    n = batch * seq
    depth = g_mix.shape[0]
    h = x.reshape(n, D_MODEL)
    tmf = FFN_ROW_TILE
    for i in range(depth):
        last = i == depth - 1
        j = i // 2
        dense = i % 2 == 0
        if dense:
            ffn_w = [w_ff_gate[j], w_ff_up[j], w_ff_down[j]]
        else:
            ffn_w = [w_e_gate[j].reshape(N_EXPERTS * D_MODEL, D_FF), w_e_up[j].reshape(N_EXPERTS * D_MODEL, D_FF),
                     w_e_down[j].reshape(N_EXPERTS * D_FF, D_MODEL)]
        hcat, (w_o, w_pg, w_pp, w_g, w_u, w_d) = _mixer_layer(
            h, g_mix[i], w_in[i], b_in[i], w_conv_qk[i], b_conv_qk[i], g_mh[i], w_conv_r[i], b_conv_r[i],
            w_ra[i], b_ra[i], w_ri[i], b_ri[i], lam[i], g_r[i], batch, seq,
            [w_out[i], w_ple_gate[i], w_ple_proj[i]] + ffn_w)
        p_i = p[i].reshape(n, D_PLE)
        if dense:
            h1, hn = _outproj_norm(h, hcat, w_o, g_ffn[i][None, :], BF16)
            y = _dense_ffn(hn, w_g, w_u, w_d)
            h = _dense_tail(h1, y, p_i, g_ple[i][None, :], w_pg, w_pp, g_final[None, :], last)
        else:
            w_r = jnp.zeros((D_MODEL, LANES), F32).at[:, :N_EXPERTS].set(w_router[j])
            h1, hn, ri, rf, cnt = _outproj_router(h, hcat, w_o, g_ffn[i][None, :], w_r)
            src, dst, tile_expert, n_tiles = _expert_tiles(ri[:, :TOP_K], cnt[0, :N_EXPERTS], n)
            ys = _moe_ffn(hn, w_g.reshape(N_EXPERTS, D_MODEL, D_FF), w_u.reshape(N_EXPERTS, D_MODEL, D_FF),
                          w_d.reshape(N_EXPERTS, D_FF, D_MODEL), src, dst, tile_expert, n_tiles)
            h = _combine(h1, rf, ys, p_i, g_ple[i][None, :], w_pg, w_pp, g_final[None, :], last)
    return h.reshape(batch, seq, D_MODEL)
```

```python
import functools

import jax
import jax.numpy as jnp
from jax import lax
from jax.experimental import pallas as pl
from jax.experimental.pallas import tpu as pltpu

F32 = jnp.float32
BF16 = jnp.bfloat16

D_MODEL = 1024
NH_M = 4
DH_M = 128
D_M = NH_M * DH_M
D_R = D_MODEL - D_M
NB_R = 8
DB_R = D_R // NB_R
CONV_W = 4
LRU_C = 8.0
D_FF = 2816
N_EXPERTS = 8
TOP_K = 2
D_PLE = 256
EPS = 1e-6

LANES = 128
SUBLANES = 8
V7X_VMEM_BYTES = 64 * 1024 * 1024

ROW_TILE = 512
MIX_CHUNK = 128
FFN_ROW_TILE = 512
FFN_COL_CHUNK = 256
INPROJ_COL_CHUNK = 256
INPROJ_ROW_TILE = 512


def _vmem_limit(nbytes):
    return int(min(nbytes, V7X_VMEM_BYTES - 6 * 1024 * 1024))


def _rmsnorm(x, g):
    return x * lax.rsqrt(jnp.mean(x * x, axis=-1, keepdims=True) + EPS) * g


def _sigmoid(x):
    return 1.0 / (1.0 + jnp.exp(-x))


def _silu(x):
    return x * _sigmoid(x)


def _causal_conv(x, prev, w, b, stage_ref):
    t = x.shape[0]
    stage_ref[0:SUBLANES, :] = prev
    stage_ref[SUBLANES:, :] = x
    acc = x * w[CONV_W - 1:CONV_W, :] + b
    for j in range(CONV_W - 1):
        lo = SUBLANES - (CONV_W - 1) + j
        acc = acc + stage_ref[lo:lo + t, :] * w[j:j + 1, :]
    return acc


def _inproj_kernel(x_ref, g_ref, w_ref, b_ref, wg_ref, bg_ref, wcqk_ref, bcqk_ref, wcr_ref, bcr_ref,
                   wbd_ref, bbd_ref, lam_ref,
                   q_ref, k_ref, v_ref, og_ref, a_ref, u_ref, gy_ref, gt_ref,
                   pqk_ref, pxr_ref, xr_ref, cstage_ref, *, tiles_per_seq):
    tm = x_ref.shape[0]

    @pl.when(pl.program_id(0) % tiles_per_seq == 0)
    def _():
        pqk_ref[...] = jnp.zeros_like(pqk_ref)
        pxr_ref[...] = jnp.zeros_like(pxr_ref)

    xn = _rmsnorm(x_ref[...], g_ref[...]).astype(BF16)
    gt_ref[...] = lax.dot_general(wg_ref[...], xn, (((1,), (1,)), ((), ())),
                                  preferred_element_type=F32) + bg_ref[...]
    cw = INPROJ_COL_CHUNK

    n_stage = [0]

    def next_stage():
        n_stage[0] += 1
        return cstage_ref.at[n_stage[0] % 2]

    def post_qk(z, c0):
        cols = slice(c0, c0 + cw)
        y = _silu(_causal_conv(z, pqk_ref[:, cols], wcqk_ref[:, cols], bcqk_ref[:, cols], next_stage()))
        pqk_ref[:, cols] = z[tm - SUBLANES:, :]
        if c0 < D_M:
            q_ref[:, cols] = (y * (DH_M ** -0.5)).astype(BF16)
        else:
            k_ref[:, c0 - D_M:c0 - D_M + cw] = y.astype(BF16)

    def post_v(z, c0):
        v_ref[:, c0:c0 + cw] = z.astype(BF16)

    def post_og(z, c0):
        og_ref[:, c0:c0 + cw] = _sigmoid(z).astype(BF16)

    def post_xr(z, c0):
        cols = slice(c0, c0 + cw)
        xr_ref[:, cols] = _causal_conv(z, pxr_ref[:, cols], wcr_ref[:, cols], bcr_ref[:, cols],
                                       next_stage())
        pxr_ref[:, cols] = z[tm - SUBLANES:, :]

    def post_gy(z, c0):
        gy_ref[:, c0:c0 + cw] = (
            0.5 * z * (1.0 + jnp.tanh(0.7978845608028654 * (z + 0.044715 * z * z * z)))).astype(BF16)

    nlam = -lam_ref[...]
    softplus = jnp.maximum(nlam, 0.0) + jnp.log(1.0 + jnp.exp(-jnp.abs(nlam)))

    def project(col):
        return lambda: (jnp.dot(xn, w_ref[:, col:col + cw], preferred_element_type=F32)
                        + b_ref[:, col:col + cw])

    def gate_matmuls(c0):
        def run():
            xr_bf = xr_ref[...].astype(BF16)
            return tuple(jnp.dot(xr_bf, wbd_ref[:, g0 + c0:g0 + c0 + cw], preferred_element_type=F32)
                         + bbd_ref[:, g0 + c0:g0 + c0 + cw] for g0 in (0, D_R))
        return run

    def post_gates(z, c0):
        cols = slice(c0, c0 + cw)
        log_a = (-LRU_C) * _sigmoid(z[0]) * softplus[:, cols]
        a_ref[:, cols] = jnp.exp(log_a)
        u_ref[:, cols] = jnp.sqrt(1.0 - jnp.exp(2.0 * log_a)) * (_sigmoid(z[1]) * xr_ref[:, cols])

    stages = []
    for base, width, post in ((0, 2 * D_M, post_qk), (2 * D_M, D_M, post_v), (3 * D_M, D_M, post_og),
                              (4 * D_M, D_R, post_xr), (4 * D_M + D_R, D_R, post_gy)):
        stages += [(project(base + c0), c0, post) for c0 in range(0, width, cw)]
    stages += [(gate_matmuls(c0), c0, post_gates) for c0 in range(0, D_R, cw)]

    z_next = stages[0][0]()
    for idx, (_, c0, post) in enumerate(stages):
        z = z_next
        if idx + 1 < len(stages):
            z_next = stages[idx + 1][0]()
        post(z, c0)


def _norm_inproj(h, g, w_main, b_main, w_gt, b_gt, wcqk, bcqk, wcr, bcr, wbd, bbd, lam, seq):
    n = h.shape[0]
    tm = INPROJ_ROW_TILE
    wcols = w_main.shape[1]
    row = lambda i: (i, 0)
    const = lambda i: (0, 0)
    return pl.pallas_call(
        functools.partial(_inproj_kernel, tiles_per_seq=seq // tm),
        grid=(n // tm,),
        in_specs=[
            pl.BlockSpec((tm, D_MODEL), row),
            pl.BlockSpec((1, D_MODEL), const),
            pl.BlockSpec((D_MODEL, wcols), const),
            pl.BlockSpec((1, wcols), const),
            pl.BlockSpec((2 * NH_M, D_MODEL), const),
            pl.BlockSpec((2 * NH_M, 1), const),
            pl.BlockSpec((CONV_W, 2 * D_M), const),
            pl.BlockSpec((1, 2 * D_M), const),
            pl.BlockSpec((CONV_W, D_R), const),
            pl.BlockSpec((1, D_R), const),
            pl.BlockSpec((D_R, 2 * D_R), const),
            pl.BlockSpec((1, 2 * D_R), const),
            pl.BlockSpec((1, D_R), const),
        ],
        out_specs=[
            pl.BlockSpec((tm, D_M), row),
            pl.BlockSpec((tm, D_M), row),
            pl.BlockSpec((tm, D_M), row),
            pl.BlockSpec((tm, D_M), row),
            pl.BlockSpec((tm, D_R), row),
            pl.BlockSpec((tm, D_R), row),
            pl.BlockSpec((tm, D_R), row),
            pl.BlockSpec((2 * NH_M, tm), lambda i: (0, i)),
        ],
        out_shape=[
            jax.ShapeDtypeStruct((n, D_M), BF16),
            jax.ShapeDtypeStruct((n, D_M), BF16),
            jax.ShapeDtypeStruct((n, D_M), BF16),
            jax.ShapeDtypeStruct((n, D_M), BF16),
            jax.ShapeDtypeStruct((n, D_R), F32),
            jax.ShapeDtypeStruct((n, D_R), F32),
            jax.ShapeDtypeStruct((n, D_R), BF16),
            jax.ShapeDtypeStruct((2 * NH_M, n), F32),
        ],
        scratch_shapes=[pltpu.VMEM((SUBLANES, 2 * D_M), F32), pltpu.VMEM((SUBLANES, D_R), F32),
                        pltpu.VMEM((tm, D_R), F32),
                        pltpu.VMEM((2, tm + SUBLANES, INPROJ_COL_CHUNK), F32)],
        compiler_params=pltpu.CompilerParams(
            dimension_semantics=("arbitrary",), vmem_limit_bytes=_vmem_limit(48 << 20)),
        name="norm_inproj",
    )(h, g, w_main, b_main, w_gt, b_gt, wcqk, bcqk, wcr, bcr, wbd, bbd, lam)


def _mixer_kernel(q_ref, k_ref, v_ref, og_ref, a_ref, u_ref, gy_ref, gt_ref, gmh_ref, gr_ref, ones_ref,
                  *rest, n_cast):
    cast_in, (out_ref,), cast_out = rest[:n_cast], rest[n_cast:n_cast + 1], rest[n_cast + 1:2 * n_cast + 1]
    s_ref, m_ref, hr_ref = rest[2 * n_cast + 1:]
    tm = q_ref.shape[0]
    for src, dst in zip(cast_in, cast_out):
        dst[...] = src[...].astype(dst.dtype)

    @pl.when(pl.program_id(1) == 0)
    def _():
        s_ref[...] = jnp.zeros_like(s_ref)
        m_ref[...] = jnp.zeros_like(m_ref)
        hr_ref[...] = jnp.zeros_like(hr_ref)

    gt = gt_ref[...]
    logf = jnp.minimum(gt, 0.0) - jnp.log(1.0 + jnp.exp(-jnp.abs(gt)))
    L = MIX_CHUNK
    ri = lax.broadcasted_iota(jnp.int32, (L, L), 0)
    ci = lax.broadcasted_iota(jnp.int32, (L, L), 1)
    tril = ci <= ri
    upper = (ri <= ci).astype(F32)
    ones_blk = jnp.ones((L, DH_M), BF16)
    neg_inf = -jnp.inf

    b_all = jnp.concatenate(
        [jnp.dot(logf[:, c * L:(c + 1) * L], upper, preferred_element_type=F32,
                 precision=lax.Precision.HIGHEST) for c in range(tm // L)], axis=1)

    for h in range(NH_M):
        s_state = s_ref[h]
        m_prev = m_ref[h]
        for c in range(tm // L):
            r0 = c * L
            q = q_ref[r0:r0 + L, h * DH_M:(h + 1) * DH_M]
            kt32 = k_ref[r0:r0 + L, h * DH_M:(h + 1) * DH_M].astype(F32).T
            v_aug = jnp.concatenate([v_ref[r0:r0 + L, h * DH_M:(h + 1) * DH_M], ones_blk], axis=1)
            ig_row = gt[h:h + 1, r0:r0 + L]
            lf_row = logf[NH_M + h:NH_M + h + 1, r0:r0 + L]
            b_row = b_all[NH_M + h:NH_M + h + 1, r0:r0 + L]
            c_row = ig_row - b_row
            g_col = jnp.maximum(jnp.max(jnp.where(tril, c_row, neg_inf), axis=-1, keepdims=True), m_prev)
            b_col = jnp.sum(jnp.where(tril, lf_row, 0.0), axis=-1, keepdims=True)
            g_last = g_col[L - 1:L, :]
            b_last = b_col[L - 1:L, :]
            dmat = jnp.exp(jnp.where(tril, c_row - g_col, neg_inf))
            s = jnp.dot(q, kt32.astype(BF16), preferred_element_type=F32) * dmat
            inter = jnp.exp(m_prev - g_col)
            lhs = jnp.concatenate([s.astype(BF16), (q.astype(F32) * inter).astype(BF16)], axis=1)
            rhs = jnp.concatenate([v_aug, s_state.astype(BF16)], axis=0)
            out = jnp.dot(lhs, rhs, preferred_element_type=F32)
            num = out[:, :DH_M]
            den = out[:, DH_M:]
            hm = num / jnp.maximum(jnp.abs(den), jnp.exp(-(b_col + g_col)))
            hm = hm * og_ref[r0:r0 + L, h * DH_M:(h + 1) * DH_M].astype(F32)
            hm = hm * lax.rsqrt(jnp.mean(hm * hm, axis=-1, keepdims=True) + EPS)
            out_ref[r0:r0 + L, h * DH_M:(h + 1) * DH_M] = (
                hm * gmh_ref[:, h * DH_M:(h + 1) * DH_M]).astype(out_ref.dtype)
            wk_row = jnp.exp(c_row - g_last)
            decay = jnp.exp(m_prev - g_last)
            s_state = decay * s_state + jnp.dot((kt32 * wk_row).astype(BF16), v_aug,
                                                preferred_element_type=F32)
            m_prev = b_last + g_last
        s_ref[h] = s_state
        m_ref[h] = m_prev

    groups = tm // SUBLANES
    a = a_ref[...].reshape(groups, SUBLANES, D_R)
    u = u_ref[...].reshape(groups, SUBLANES, D_R)
    row8 = lax.broadcasted_iota(jnp.int32, (groups, SUBLANES, D_R), 1)
    for sft in (1, 2, 4):
        keep = row8 >= sft
        a_sh = pltpu.roll(a, sft, axis=1)
        u_sh = pltpu.roll(u, sft, axis=1)
        u = jnp.where(keep, a * u_sh + u, u)
        a = jnp.where(keep, a * a_sh, a)
    hprev = hr_ref[...]
    rows = []
    for gi in range(groups):
        blk = u[gi] + a[gi] * hprev
        rows.append(blk)
        hprev = blk[SUBLANES - 1:SUBLANES, :]
    hr_ref[...] = hprev
    hr = jnp.concatenate(rows, axis=0) * gy_ref[...].astype(F32)
    sq = hr * hr
    sq_hi = sq.astype(BF16)
    sq_lo = (sq - sq_hi.astype(F32)).astype(BF16)
    gsum = (jnp.dot(sq_hi, ones_ref[...], preferred_element_type=F32)
            + jnp.dot(sq_lo, ones_ref[...], preferred_element_type=F32))
    hr = hr * lax.rsqrt(gsum * (1.0 / DB_R) + EPS) * gr_ref[...]
    out_ref[:, D_M:] = hr.astype(out_ref.dtype)


def _cast_slices(w, steps):
    rows, cols = w.shape
    used = max(d for d in range(1, steps + 1)
               if steps % d == 0 and rows % d == 0 and (rows // d) % (2 * SUBLANES) == 0)
    per = steps // used
    return (rows // used, cols), per


def _mixer(q, k, v, og, a, u, gy, gt, gmh, gr, ones_bd, batch, seq, to_bf16=()):
    tm = ROW_TILE
    spb = seq // tm
    row = lambda b, s: (b * spb + s, 0)
    const = lambda b, s: (0, 0)
    n = batch * seq
    cast_specs = []
    for w in to_bf16:
        blk, per = _cast_slices(w, batch * spb)
        cast_specs.append(pl.BlockSpec(blk, functools.partial(
            lambda b, s, per: ((b * spb + s) // per, 0), per=per)))
    outs = pl.pallas_call(
        functools.partial(_mixer_kernel, n_cast=len(to_bf16)),
        grid=(batch, spb),
        in_specs=[
            pl.BlockSpec((tm, D_M), row),
            pl.BlockSpec((tm, D_M), row),
            pl.BlockSpec((tm, D_M), row),
            pl.BlockSpec((tm, D_M), row),
            pl.BlockSpec((tm, D_R), row),
            pl.BlockSpec((tm, D_R), row),
            pl.BlockSpec((tm, D_R), row),
            pl.BlockSpec((2 * NH_M, tm), lambda b, s: (0, b * spb + s)),
            pl.BlockSpec((1, D_M), const),
            pl.BlockSpec((1, D_R), const),
            pl.BlockSpec((D_R, D_R), const),
        ] + cast_specs,
        out_specs=[pl.BlockSpec((tm, D_MODEL), row)] + cast_specs,
        out_shape=[jax.ShapeDtypeStruct((n, D_MODEL), BF16)]
        + [jax.ShapeDtypeStruct(w.shape, BF16) for w in to_bf16],
        scratch_shapes=[
            pltpu.VMEM((NH_M, DH_M, 2 * DH_M), F32),
            pltpu.VMEM((NH_M, 1, 1), F32),
            pltpu.VMEM((1, D_R), F32),
        ],
        compiler_params=pltpu.CompilerParams(
            dimension_semantics=("arbitrary", "arbitrary"), vmem_limit_bytes=_vmem_limit(58 << 20)),
        name="seq_mixer",
    )(q, k, v, og, a, u, gy, gt, gmh, gr, ones_bd, *to_bf16)
    return outs[0], outs[1:]


def _outproj_kernel(h_ref, hc_ref, w_ref, g_ref, h1_ref, hn_ref):
    h1 = h_ref[...] + jnp.dot(hc_ref[...], w_ref[...], preferred_element_type=F32)
    h1_ref[...] = h1
    hn_ref[...] = _rmsnorm(h1, g_ref[...]).astype(hn_ref.dtype)


def _outproj_norm(h, hcat, w_out, g, hn_dtype):
    n = h.shape[0]
    tm = ROW_TILE
    row = lambda i: (i, 0)
    const = lambda i: (0, 0)
    return pl.pallas_call(
        _outproj_kernel,
        grid=(n // tm,),
        in_specs=[
            pl.BlockSpec((tm, D_MODEL), row),
            pl.BlockSpec((tm, D_MODEL), row),
            pl.BlockSpec((D_MODEL, D_MODEL), const),
            pl.BlockSpec((1, D_MODEL), const),
        ],
        out_specs=[pl.BlockSpec((tm, D_MODEL), row), pl.BlockSpec((tm, D_MODEL), row)],
        out_shape=[jax.ShapeDtypeStruct((n, D_MODEL), F32), jax.ShapeDtypeStruct((n, D_MODEL), hn_dtype)],
        compiler_params=pltpu.CompilerParams(
            dimension_semantics=("arbitrary",), vmem_limit_bytes=_vmem_limit(32 << 20)),
        name="outproj_norm",
    )(h, hcat, w_out, g)


def _outproj_router_kernel(h_ref, hc_ref, w_ref, g_ref, wr_ref, h1_ref, hn_ref, ri_ref, rf_ref, cnt_ref,
                           carry_ref):
    tm = h_ref.shape[0]

    @pl.when(pl.program_id(0) == 0)
    def _():
        carry_ref[...] = jnp.zeros_like(carry_ref)

    h1 = h_ref[...] + jnp.dot(hc_ref[...], w_ref[...], preferred_element_type=F32)
    h1_ref[...] = h1
    hn = _rmsnorm(h1, g_ref[...])
    _store_rows_3d(hn_ref, hn)
    lane_i = lax.broadcasted_iota(jnp.int32, (tm, LANES), 1)
    lane = lane_i.astype(F32)
    hn_hi = hn.astype(BF16)
    hn_lo = (hn - hn_hi.astype(F32)).astype(BF16)
    wr = wr_ref[...]
    wr_hi = wr.astype(BF16)
    wr_lo = (wr - wr_hi.astype(F32)).astype(BF16)
    logits = (jnp.dot(hn_hi, wr_hi, preferred_element_type=F32)
              + (jnp.dot(hn_hi, wr_lo, preferred_element_type=F32)
                 + jnp.dot(hn_lo, wr_hi, preferred_element_type=F32)))
    logits = jnp.where(lane_i < N_EXPERTS, logits, -jnp.inf)
    v1 = jnp.max(logits, axis=-1, keepdims=True)
    i1 = jnp.min(jnp.where(logits == v1, lane, float(LANES)), axis=-1, keepdims=True)
    m1 = lane == i1
    rest = jnp.where(m1, -jnp.inf, logits)
    v2 = jnp.max(rest, axis=-1, keepdims=True)
    i2 = jnp.min(jnp.where(rest == v2, lane, float(LANES)), axis=-1, keepdims=True)
    m2 = lane == i2
    e2 = jnp.exp(v2 - v1)
    g1 = 1.0 / (1.0 + e2)
    g2 = e2 / (1.0 + e2)
    carry = carry_ref[...] + jnp.sum(jnp.where(m1 | m2, 1.0, 0.0), axis=0, keepdims=True)
    carry_ref[...] = carry
    ri_ref[...] = jnp.where(lane_i == 0, i1.astype(jnp.int32), i2.astype(jnp.int32))
    rf_ref[...] = jnp.where(lane_i == 0, g1, g2)
    cnt_ref[...] = jnp.broadcast_to(carry, cnt_ref.shape)


def _outproj_router(h, hcat, w_out, g, w_router_pad):
    n = h.shape[0]
    tm = ROW_TILE
    row = lambda i: (i, 0)
    const = lambda i: (0, 0)
    return pl.pallas_call(
        _outproj_router_kernel,
        grid=(n // tm,),
        in_specs=[
            pl.BlockSpec((tm, D_MODEL), row),
            pl.BlockSpec((tm, D_MODEL), row),
            pl.BlockSpec((D_MODEL, D_MODEL), const),
            pl.BlockSpec((1, D_MODEL), const),
            pl.BlockSpec((D_MODEL, LANES), const),
        ],
        out_specs=[pl.BlockSpec((tm, D_MODEL), row),
                   pl.BlockSpec((tm, ROW_SLABS, LANES), lambda i: (i, 0, 0)),
                   pl.BlockSpec((tm, LANES), row), pl.BlockSpec((tm, LANES), row),
                   pl.BlockSpec((SUBLANES, LANES), const)],
        out_shape=[jax.ShapeDtypeStruct((n, D_MODEL), F32),
                   jax.ShapeDtypeStruct((n, ROW_SLABS, LANES), F32),
                   jax.ShapeDtypeStruct((n, LANES), jnp.int32), jax.ShapeDtypeStruct((n, LANES), F32),
                   jax.ShapeDtypeStruct((SUBLANES, LANES), F32)],
        scratch_shapes=[pltpu.VMEM((1, LANES), F32)],
        compiler_params=pltpu.CompilerParams(
            dimension_semantics=("arbitrary",), vmem_limit_bytes=_vmem_limit(40 << 20)),
        name="outproj_router",
    )(h, hcat, w_out, g, w_router_pad)


def _swiglu(x, wg_ref, wu_ref, wd_ref, act_ref):
    for f0 in range(0, D_FF, FFN_COL_CHUNK):
        g = jnp.dot(x, wg_ref[:, f0:f0 + FFN_COL_CHUNK], preferred_element_type=F32)
        u = jnp.dot(x, wu_ref[:, f0:f0 + FFN_COL_CHUNK], preferred_element_type=F32)
        act_ref[:, f0:f0 + FFN_COL_CHUNK] = (_silu(g) * u).astype(BF16)
    return jnp.dot(act_ref[...], wd_ref[...], preferred_element_type=F32)


def _dense_ffn_kernel(x_ref, wg_ref, wu_ref, wd_ref, y_ref, act_ref):
    y_ref[...] = _swiglu(x_ref[...], wg_ref, wu_ref, wd_ref, act_ref)


def _dense_ffn(x, w_gate, w_up, w_down):
    n = x.shape[0]
    tm = FFN_ROW_TILE
    row = lambda i: (i, 0)
    const = lambda i: (0, 0)
    return pl.pallas_call(
        _dense_ffn_kernel,
        grid=(n // tm,),
        in_specs=[
            pl.BlockSpec((tm, D_MODEL), row),
            pl.BlockSpec((D_MODEL, D_FF), const),
            pl.BlockSpec((D_MODEL, D_FF), const),
            pl.BlockSpec((D_FF, D_MODEL), const),
        ],
        out_specs=pl.BlockSpec((tm, D_MODEL), row),
        out_shape=jax.ShapeDtypeStruct((n, D_MODEL), F32),
        scratch_shapes=[pltpu.VMEM((tm, D_FF), BF16)],
        compiler_params=pltpu.CompilerParams(
            dimension_semantics=("arbitrary",), vmem_limit_bytes=_vmem_limit(58 << 20)),
        name="dense_swiglu",
    )(x, w_gate, w_up, w_down)


KEY_STRIDE = 1 << 16

ROW_SLABS = D_MODEL // LANES


def _rows_to_2d(ref3):
    return pltpu.einshape("rsl->r(sl)", ref3[...])


def _store_rows_3d(ref3, val):
    ref3[...] = pltpu.einshape("r(sl)->rsl", val, s=ROW_SLABS)


ROW_COPY_UNROLL = 8


def _moe_ffn_kernel(te_ref, nt_ref, src_ref, src_next_ref, dst_ref, hn_ref, wg_ref, wu_ref, wd_ref,
                    out_ref, xbuf, ybuf, act_ref, sem_g, sem_s, *, n_tok):
    i = pl.program_id(0)
    nt = nt_ref[0]
    tmf = xbuf.shape[1]
    slot = i % 2
    dump0 = TOP_K * n_tok

    def gather(sref, slot_):
        def body(j, carry):
            for u in range(ROW_COPY_UNROLL):
                r = ROW_COPY_UNROLL * j + u
                pltpu.make_async_copy(hn_ref.at[sref[0, 0, r]], xbuf.at[slot_, r],
                                      sem_g.at[slot_]).start(priority=u % 2)
            return carry
        lax.fori_loop(0, tmf // ROW_COPY_UNROLL, body, 0)

    def wait_rows(sem):
        pltpu.make_async_copy(hn_ref.at[pl.ds(0, tmf)], xbuf.at[0], sem).wait()

    @pl.when(i == 0)
    def _():
        xbuf[0] = jnp.zeros(xbuf.shape[1:], xbuf.dtype)
        for s in range(2):
            fill = pltpu.make_async_copy(xbuf.at[0], out_ref.at[pl.ds(dump0 + s * tmf, tmf)], sem_s.at[s])
            fill.start()
            fill.wait()
        gather(src_ref, 0)

    @pl.when(i + 1 < nt)
    def _():
        gather(src_next_ref, 1 - slot)

    @pl.when(i < nt)
    def _():
        wait_rows(sem_g.at[slot])
        y = _swiglu(_rows_to_2d(xbuf.at[slot]).astype(BF16), wg_ref, wu_ref, wd_ref, act_ref)

        @pl.when(i >= 2)
        def _():
            wait_rows(sem_s.at[slot])

        _store_rows_3d(ybuf.at[slot], y)

        def body(j, carry):
            for u in range(ROW_COPY_UNROLL):
                r = ROW_COPY_UNROLL * j + u
                pltpu.make_async_copy(ybuf.at[slot, r], out_ref.at[dst_ref[0, 0, r]],
                                      sem_s.at[slot]).start(priority=u % 2)
            return carry
        lax.fori_loop(0, tmf // ROW_COPY_UNROLL, body, 0)

    @pl.when(i == pl.num_programs(0) - 1)
    def _():
        @pl.when(nt >= 2)
        def _():
            wait_rows(sem_s.at[nt % 2])
        wait_rows(sem_s.at[(nt - 1) % 2])


def _moe_ffn(hn, w_gate, w_up, w_down, src_tiles, dst_tiles, tile_expert, n_tiles):
    n_tok = hn.shape[0]
    tmf = FFN_ROW_TILE
    t_total = src_tiles.shape[0]
    grid_spec = pltpu.PrefetchScalarGridSpec(
        num_scalar_prefetch=2,
        grid=(t_total,),
        in_specs=[
            pl.BlockSpec((1, 1, tmf), lambda i, te, nt: (i, 0, 0), memory_space=pltpu.SMEM),
            pl.BlockSpec((1, 1, tmf), lambda i, te, nt: (jnp.minimum(i + 1, t_total - 1), 0, 0),
                         memory_space=pltpu.SMEM),
            pl.BlockSpec((1, 1, tmf), lambda i, te, nt: (i, 0, 0), memory_space=pltpu.SMEM),
            pl.BlockSpec(memory_space=pl.ANY),
            pl.BlockSpec((None, D_MODEL, D_FF), lambda i, te, nt: (te[i], 0, 0)),
            pl.BlockSpec((None, D_MODEL, D_FF), lambda i, te, nt: (te[i], 0, 0)),
            pl.BlockSpec((None, D_FF, D_MODEL), lambda i, te, nt: (te[i], 0, 0)),
        ],
        out_specs=pl.BlockSpec(memory_space=pl.ANY),
        scratch_shapes=[
            pltpu.VMEM((2, tmf, ROW_SLABS, LANES), F32),
            pltpu.VMEM((2, tmf, ROW_SLABS, LANES), F32),
            pltpu.VMEM((tmf, D_FF), BF16),
            pltpu.SemaphoreType.DMA((2,)),
            pltpu.SemaphoreType.DMA((2,)),
        ],
    )
    return pl.pallas_call(
        functools.partial(_moe_ffn_kernel, n_tok=n_tok),
        grid_spec=grid_spec,
        out_shape=jax.ShapeDtypeStruct((TOP_K * n_tok + 2 * tmf, ROW_SLABS, LANES), F32),
        compiler_params=pltpu.CompilerParams(
            dimension_semantics=("arbitrary",), vmem_limit_bytes=_vmem_limit(58 << 20),
            has_side_effects=True),
        name="expert_swiglu",
    )(tile_expert, n_tiles, src_tiles, src_tiles, dst_tiles, hn, w_gate, w_up, w_down)


def _ple_tail(h2, p_tile, g_ple, wpg_ref, wpp_ref):
    gate = _sigmoid(jnp.dot(_rmsnorm(h2, g_ple).astype(BF16), wpg_ref[...], preferred_element_type=F32))
    return h2 + gate * jnp.dot(p_tile.astype(BF16), wpp_ref[...], preferred_element_type=F32)


def _combine_kernel(h1_ref, rf_ref, y0_ref, y1_ref, p_ref, gple_ref, wpg_ref, wpp_ref, gfin_ref,
                    out_ref, *, final_norm):
    rf = rf_ref[...]
    h2 = h1_ref[...] + rf[:, 0:1] * _rows_to_2d(y0_ref) + rf[:, 1:2] * _rows_to_2d(y1_ref)
    h3 = _ple_tail(h2, p_ref[...], gple_ref[...], wpg_ref, wpp_ref)
    out_ref[...] = _rmsnorm(h3, gfin_ref[...]) if final_norm else h3


def _combine(h1, rf, ys, p, g_ple, w_pg, w_pp, g_final, final_norm):
    n = h1.shape[0]
    tm = ROW_TILE
    row = lambda i: (i, 0)
    const = lambda i: (0, 0)
    return pl.pallas_call(
        functools.partial(_combine_kernel, final_norm=final_norm),
        grid=(n // tm,),
        in_specs=[
            pl.BlockSpec((tm, D_MODEL), row),
            pl.BlockSpec((tm, LANES), row),
            pl.BlockSpec((tm, ROW_SLABS, LANES), lambda i: (i, 0, 0)),
            pl.BlockSpec((tm, ROW_SLABS, LANES), lambda i: (n // tm + i, 0, 0)),
            pl.BlockSpec((tm, D_PLE), row),
            pl.BlockSpec((1, D_MODEL), const),
            pl.BlockSpec((D_MODEL, D_MODEL), const),
            pl.BlockSpec((D_PLE, D_MODEL), const),
            pl.BlockSpec((1, D_MODEL), const),
        ],
        out_specs=pl.BlockSpec((tm, D_MODEL), row),
        out_shape=jax.ShapeDtypeStruct((n, D_MODEL), F32),
        compiler_params=pltpu.CompilerParams(
            dimension_semantics=("arbitrary",), vmem_limit_bytes=_vmem_limit(40 << 20)),
        name="expert_combine_tail",
    )(h1, rf, ys, ys, p, g_ple, w_pg, w_pp, g_final)


def _dense_tail_kernel(h1_ref, y_ref, p_ref, gple_ref, wpg_ref, wpp_ref, gfin_ref, out_ref, *,
                       final_norm):
    h2 = h1_ref[...] + y_ref[...]
    h3 = _ple_tail(h2, p_ref[...], gple_ref[...], wpg_ref, wpp_ref)
    out_ref[...] = _rmsnorm(h3, gfin_ref[...]) if final_norm else h3


def _dense_tail(h1, y, p, g_ple, w_pg, w_pp, g_final, final_norm):
    n = h1.shape[0]
    tm = ROW_TILE
    row = lambda i: (i, 0)
    const = lambda i: (0, 0)
    return pl.pallas_call(
        functools.partial(_dense_tail_kernel, final_norm=final_norm),
        grid=(n // tm,),
        in_specs=[
            pl.BlockSpec((tm, D_MODEL), row),
            pl.BlockSpec((tm, D_MODEL), row),
            pl.BlockSpec((tm, D_PLE), row),
            pl.BlockSpec((1, D_MODEL), const),
            pl.BlockSpec((D_MODEL, D_MODEL), const),
            pl.BlockSpec((D_PLE, D_MODEL), const),
            pl.BlockSpec((1, D_MODEL), const),
        ],
        out_specs=pl.BlockSpec((tm, D_MODEL), row),
        out_shape=jax.ShapeDtypeStruct((n, D_MODEL), F32),
        compiler_params=pltpu.CompilerParams(
            dimension_semantics=("arbitrary",), vmem_limit_bytes=_vmem_limit(32 << 20)),
        name="dense_ple_tail",
    )(h1, y, p, g_ple, w_pg, w_pp, g_final)


def _block_diag(w):
    nb, db, _ = w.shape
    eye = jnp.eye(nb, dtype=w.dtype)
    return (eye[:, None, :, None] * w[:, :, None, :]).reshape(nb * db, nb * db)


def _mixer_layer(h, g_mix, w_in, b_in, w_conv_qk, b_conv_qk, g_mh, w_conv_r, b_conv_r,
                 w_ra, b_ra, w_ri, b_ri, lam, g_r, batch, seq, to_bf16):
    q_end, k_end, v_end, o_end = D_M, 2 * D_M, 3 * D_M, 4 * D_M
    i_end = o_end + NH_M
    f_end = i_end + NH_M
    xr_end = f_end + D_R
    w_main = jnp.concatenate([w_in[:, :o_end], w_in[:, f_end:]], axis=1).astype(BF16)
    b_main = jnp.concatenate([b_in[:o_end], b_in[f_end:]])[None, :]
    w_gt = w_in[:, o_end:f_end].T.astype(BF16)
    b_gt = b_in[o_end:f_end][:, None]
    del q_end, k_end, v_end, xr_end
    wbd = jnp.concatenate([_block_diag(w_ra), _block_diag(w_ri)], axis=1).astype(BF16)
    bbd = jnp.concatenate([b_ra, b_ri])[None, :]
    ones_bd = _block_diag(jnp.ones((NB_R, DB_R, DB_R), F32)).astype(BF16)
    q, k, v, og, a, u, gy, gt = _norm_inproj(
        h, g_mix[None, :], w_main, b_main, w_gt, b_gt, w_conv_qk, b_conv_qk[None, :],
        w_conv_r, b_conv_r[None, :], wbd, bbd, lam[None, :], seq)
    return _mixer(q, k, v, og, a, u, gy, gt, g_mh[None, :], g_r[None, :], ones_bd, batch, seq, to_bf16)


def _expert_tiles(top_idx, counts_f, n_tok):
    tmf = FFN_ROW_TILE
    counts = counts_f.astype(jnp.int32)
    padded = (counts + tmf - 1) // tmf * tmf
    n_tiles = (jnp.sum(padded) // tmf).astype(jnp.int32).reshape(1)
    a_ids = jnp.arange(TOP_K * n_tok, dtype=jnp.int32)
    real = top_idx.reshape(-1) * KEY_STRIDE + a_ids
    pad_j = jnp.arange(tmf, dtype=jnp.int32)[None, :]
    experts = jnp.arange(N_EXPERTS, dtype=jnp.int32)[:, None]
    pad_keys = jnp.where(pad_j < (padded - counts)[:, None],
                         experts * KEY_STRIDE + TOP_K * n_tok + pad_j, N_EXPERTS * KEY_STRIDE)
    keys = jnp.sort(jnp.concatenate([real, pad_keys.reshape(-1)]))
    t_total = (TOP_K * n_tok) // tmf + N_EXPERTS
    a_sorted = keys & (KEY_STRIDE - 1)
    valid = a_sorted < TOP_K * n_tok
    row = jnp.arange(t_total * tmf, dtype=jnp.int32)
    dump = TOP_K * n_tok + (row // tmf % 2) * tmf + row % tmf
    src = jnp.where(valid, a_sorted >> 1, 0).reshape(t_total, 1, tmf)
    dst = jnp.where(valid, (a_sorted & 1) * n_tok + (a_sorted >> 1), dump).reshape(t_total, 1, tmf)
    tile_expert = jnp.minimum(keys[::tmf] // KEY_STRIDE, N_EXPERTS - 1).astype(jnp.int32)
    return src, dst, tile_expert, n_tiles


def kernel(x, p, g_mix, w_in, b_in, w_conv_qk, b_conv_qk, g_mh, w_conv_r, b_conv_r, w_ra, b_ra, w_ri, b_ri, lam, g_r, w_out, g_ffn, w_ff_gate, w_ff_up, w_ff_down, w_router, w_e_gate, w_e_up, w_e_down, g_ple, w_ple_gate, w_ple_proj, g_final):
    batch, seq, _ = x.shape
    n = batch * seq
    depth = g_mix.shape[0]
    h = x.reshape(n, D_MODEL)
    tmf = FFN_ROW_TILE
    for i in range(depth):
        last = i == depth - 1
        j = i // 2
        dense = i % 2 == 0
        if dense:
            ffn_w = [w_ff_gate[j], w_ff_up[j], w_ff_down[j]]
        else:
            ffn_w = [w_e_gate[j].reshape(N_EXPERTS * D_MODEL, D_FF), w_e_up[j].reshape(N_EXPERTS * D_MODEL, D_FF),
                     w_e_down[j].reshape(N_EXPERTS * D_FF, D_MODEL)]
        hcat, (w_o, w_pg, w_pp, w_g, w_u, w_d) = _mixer_layer(
            h, g_mix[i], w_in[i], b_in[i], w_conv_qk[i], b_conv_qk[i], g_mh[i], w_conv_r[i], b_conv_r[i],
            w_ra[i], b_ra[i], w_ri[i], b_ri[i], lam[i], g_r[i], batch, seq,
            [w_out[i], w_ple_gate[i], w_ple_proj[i]] + ffn_w)
        p_i = p[i].reshape(n, D_PLE)
        if dense:
            h1, hn = _outproj_norm(h, hcat, w_o, g_ffn[i][None, :], BF16)
            y = _dense_ffn(hn, w_g, w_u, w_d)
            h = _dense_tail(h1, y, p_i, g_ple[i][None, :], w_pg, w_pp, g_final[None, :], last)
        else:
            w_r = jnp.zeros((D_MODEL, LANES), F32).at[:, :N_EXPERTS].set(w_router[j])
            h1, hn, ri, rf, cnt = _outproj_router(h, hcat, w_o, g_ffn[i][None, :], w_r)
            src, dst, tile_expert, n_tiles = _expert_tiles(ri[:, :TOP_K], cnt[0, :N_EXPERTS], n)
            ys = _moe_ffn(hn, w_g.reshape(N_EXPERTS, D_MODEL, D_FF), w_u.reshape(N_EXPERTS, D_MODEL, D_FF),
                          w_d.reshape(N_EXPERTS, D_FF, D_MODEL), src, dst, tile_expert, n_tiles)
            h = _combine(h1, rf, ys, p_i, g_ple[i][None, :], w_pg, w_pp, g_final[None, :], last)
    return h.reshape(batch, seq, D_MODEL)
```

```python
import functools

import jax
import jax.numpy as jnp
from jax import lax
from jax.experimental import pallas as pl
from jax.experimental.pallas import tpu as pltpu

F32 = jnp.float32
BF16 = jnp.bfloat16

D_MODEL = 1024
NH_M = 4
DH_M = 128
D_M = NH_M * DH_M
D_R = D_MODEL - D_M
NB_R = 8
DB_R = D_R // NB_R
CONV_W = 4
LRU_C = 8.0
D_FF = 2816
N_EXPERTS = 8
TOP_K = 2
D_PLE = 256
EPS = 1e-6

LANES = 128
SUBLANES = 8
V7X_VMEM_BYTES = 64 * 1024 * 1024

ROW_TILE = 512
MIX_CHUNK = 128
FFN_ROW_TILE = 512
FFN_COL_CHUNK = 256
INPROJ_COL_CHUNK = 256
INPROJ_ROW_TILE = 512


def _vmem_limit(nbytes):
    return int(min(nbytes, V7X_VMEM_BYTES - 6 * 1024 * 1024))


def _rmsnorm(x, g):
    return x * lax.rsqrt(jnp.mean(x * x, axis=-1, keepdims=True) + EPS) * g


def _sigmoid(x):
    return 1.0 / (1.0 + jnp.exp(-x))


def _silu(x):
    return x * _sigmoid(x)


def _causal_conv(x, prev, w, b, stage_ref):
    t = x.shape[0]
    stage_ref[0:SUBLANES, :] = prev
    stage_ref[SUBLANES:, :] = x
    acc = x * w[CONV_W - 1:CONV_W, :] + b
    for j in range(CONV_W - 1):
        lo = SUBLANES - (CONV_W - 1) + j
        acc = acc + stage_ref[lo:lo + t, :] * w[j:j + 1, :]
    return acc


def _inproj_kernel(x_ref, g_ref, w_ref, b_ref, wg_ref, bg_ref, wcqk_ref, bcqk_ref, wcr_ref, bcr_ref,
                   wbd_ref, bbd_ref, lam_ref,
                   q_ref, k_ref, v_ref, og_ref, a_ref, u_ref, gy_ref, gt_ref,
                   pqk_ref, pxr_ref, xr_ref, cstage_ref, *, tiles_per_seq):
    tm = x_ref.shape[0]

    @pl.when(pl.program_id(0) % tiles_per_seq == 0)
    def _():
        pqk_ref[...] = jnp.zeros_like(pqk_ref)
        pxr_ref[...] = jnp.zeros_like(pxr_ref)

    xn = _rmsnorm(x_ref[...], g_ref[...]).astype(BF16)
    gt_ref[...] = lax.dot_general(wg_ref[...], xn, (((1,), (1,)), ((), ())),
                                  preferred_element_type=F32) + bg_ref[...]
    cw = INPROJ_COL_CHUNK

    n_stage = [0]

    def next_stage():
        n_stage[0] += 1
        return cstage_ref.at[n_stage[0] % 2]

    def post_qk(z, c0):
        cols = slice(c0, c0 + cw)
        y = _silu(_causal_conv(z, pqk_ref[:, cols], wcqk_ref[:, cols], bcqk_ref[:, cols], next_stage()))
        pqk_ref[:, cols] = z[tm - SUBLANES:, :]
        if c0 < D_M:
            q_ref[:, cols] = (y * (DH_M ** -0.5)).astype(BF16)
        else:
            k_ref[:, c0 - D_M:c0 - D_M + cw] = y.astype(BF16)

    def post_v(z, c0):
        v_ref[:, c0:c0 + cw] = z.astype(BF16)

    def post_og(z, c0):
        og_ref[:, c0:c0 + cw] = _sigmoid(z).astype(BF16)

    def post_xr(z, c0):
        cols = slice(c0, c0 + cw)
        xr_ref[:, cols] = _causal_conv(z, pxr_ref[:, cols], wcr_ref[:, cols], bcr_ref[:, cols],
                                       next_stage())
        pxr_ref[:, cols] = z[tm - SUBLANES:, :]

    def post_gy(z, c0):
        gy_ref[:, c0:c0 + cw] = (
            0.5 * z * (1.0 + jnp.tanh(0.7978845608028654 * (z + 0.044715 * z * z * z)))).astype(BF16)

    nlam = -lam_ref[...]
    softplus = jnp.maximum(nlam, 0.0) + jnp.log(1.0 + jnp.exp(-jnp.abs(nlam)))

    def project(col):
        return lambda: (jnp.dot(xn, w_ref[:, col:col + cw], preferred_element_type=F32)
                        + b_ref[:, col:col + cw])

    def gate_matmuls(c0):
        def run():
            xr_bf = xr_ref[...].astype(BF16)
            return tuple(jnp.dot(xr_bf, wbd_ref[:, g0 + c0:g0 + c0 + cw], preferred_element_type=F32)
                         + bbd_ref[:, g0 + c0:g0 + c0 + cw] for g0 in (0, D_R))
        return run

    def post_gates(z, c0):
        cols = slice(c0, c0 + cw)
        log_a = (-LRU_C) * _sigmoid(z[0]) * softplus[:, cols]
        a_ref[:, cols] = jnp.exp(log_a)
        u_ref[:, cols] = jnp.sqrt(1.0 - jnp.exp(2.0 * log_a)) * (_sigmoid(z[1]) * xr_ref[:, cols])

    stages = []
    for base, width, post in ((0, 2 * D_M, post_qk), (2 * D_M, D_M, post_v), (3 * D_M, D_M, post_og),
                              (4 * D_M, D_R, post_xr), (4 * D_M + D_R, D_R, post_gy)):
        stages += [(project(base + c0), c0, post) for c0 in range(0, width, cw)]
    stages += [(gate_matmuls(c0), c0, post_gates) for c0 in range(0, D_R, cw)]

    z_next = stages[0][0]()
    for idx, (_, c0, post) in enumerate(stages):
        z = z_next
        if idx + 1 < len(stages):
            z_next = stages[idx + 1][0]()
        post(z, c0)


def _norm_inproj(h, g, w_main, b_main, w_gt, b_gt, wcqk, bcqk, wcr, bcr, wbd, bbd, lam, seq):
    n = h.shape[0]
    tm = INPROJ_ROW_TILE
    wcols = w_main.shape[1]
    row = lambda i: (i, 0)
    const = lambda i: (0, 0)
    return pl.pallas_call(
        functools.partial(_inproj_kernel, tiles_per_seq=seq // tm),
        grid=(n // tm,),
        in_specs=[
            pl.BlockSpec((tm, D_MODEL), row),
            pl.BlockSpec((1, D_MODEL), const),
            pl.BlockSpec((D_MODEL, wcols), const),
            pl.BlockSpec((1, wcols), const),
            pl.BlockSpec((2 * NH_M, D_MODEL), const),
            pl.BlockSpec((2 * NH_M, 1), const),
            pl.BlockSpec((CONV_W, 2 * D_M), const),
            pl.BlockSpec((1, 2 * D_M), const),
            pl.BlockSpec((CONV_W, D_R), const),
            pl.BlockSpec((1, D_R), const),
            pl.BlockSpec((D_R, 2 * D_R), const),
            pl.BlockSpec((1, 2 * D_R), const),
            pl.BlockSpec((1, D_R), const),
        ],
        out_specs=[
            pl.BlockSpec((tm, D_M), row),
            pl.BlockSpec((tm, D_M), row),
            pl.BlockSpec((tm, D_M), row),
            pl.BlockSpec((tm, D_M), row),
            pl.BlockSpec((tm, D_R), row),
            pl.BlockSpec((tm, D_R), row),
            pl.BlockSpec((tm, D_R), row),
            pl.BlockSpec((2 * NH_M, tm), lambda i: (0, i)),
        ],
        out_shape=[
            jax.ShapeDtypeStruct((n, D_M), BF16),
            jax.ShapeDtypeStruct((n, D_M), BF16),
            jax.ShapeDtypeStruct((n, D_M), BF16),
            jax.ShapeDtypeStruct((n, D_M), BF16),
            jax.ShapeDtypeStruct((n, D_R), F32),
            jax.ShapeDtypeStruct((n, D_R), F32),
            jax.ShapeDtypeStruct((n, D_R), BF16),
            jax.ShapeDtypeStruct((2 * NH_M, n), F32),
        ],
        scratch_shapes=[pltpu.VMEM((SUBLANES, 2 * D_M), F32), pltpu.VMEM((SUBLANES, D_R), F32),
                        pltpu.VMEM((tm, D_R), F32),
                        pltpu.VMEM((2, tm + SUBLANES, INPROJ_COL_CHUNK), F32)],
        compiler_params=pltpu.CompilerParams(
            dimension_semantics=("arbitrary",), vmem_limit_bytes=_vmem_limit(48 << 20)),
        name="norm_inproj",
    )(h, g, w_main, b_main, w_gt, b_gt, wcqk, bcqk, wcr, bcr, wbd, bbd, lam)


def _mixer_kernel(q_ref, k_ref, v_ref, og_ref, a_ref, u_ref, gy_ref, gt_ref, gmh_ref, gr_ref, ones_ref,
                  *rest, n_cast):
    cast_in, (out_ref,), cast_out = rest[:n_cast], rest[n_cast:n_cast + 1], rest[n_cast + 1:2 * n_cast + 1]
    s_ref, m_ref, hr_ref = rest[2 * n_cast + 1:]
    tm = q_ref.shape[0]
    for src, dst in zip(cast_in, cast_out):
        dst[...] = src[...].astype(dst.dtype)

    @pl.when(pl.program_id(1) == 0)
    def _():
        s_ref[...] = jnp.zeros_like(s_ref)
        m_ref[...] = jnp.zeros_like(m_ref)
        hr_ref[...] = jnp.zeros_like(hr_ref)

    gt = gt_ref[...]
    logf = jnp.minimum(gt, 0.0) - jnp.log(1.0 + jnp.exp(-jnp.abs(gt)))
    L = MIX_CHUNK
    ri = lax.broadcasted_iota(jnp.int32, (L, L), 0)
    ci = lax.broadcasted_iota(jnp.int32, (L, L), 1)
    tril = ci <= ri
    upper = (ri <= ci).astype(F32)
    ones_blk = jnp.ones((L, DH_M), BF16)
    neg_inf = -jnp.inf

    b_all = jnp.concatenate(
        [jnp.dot(logf[:, c * L:(c + 1) * L], upper, preferred_element_type=F32,
                 precision=lax.Precision.HIGHEST) for c in range(tm // L)], axis=1)

    for h in range(NH_M):
        s_state = s_ref[h]
        m_prev = m_ref[h]
        for c in range(tm // L):
            r0 = c * L
            q = q_ref[r0:r0 + L, h * DH_M:(h + 1) * DH_M]
            kt32 = k_ref[r0:r0 + L, h * DH_M:(h + 1) * DH_M].astype(F32).T
            v_aug = jnp.concatenate([v_ref[r0:r0 + L, h * DH_M:(h + 1) * DH_M], ones_blk], axis=1)
            ig_row = gt[h:h + 1, r0:r0 + L]
            lf_row = logf[NH_M + h:NH_M + h + 1, r0:r0 + L]
            b_row = b_all[NH_M + h:NH_M + h + 1, r0:r0 + L]
            c_row = ig_row - b_row
            g_col = jnp.maximum(jnp.max(jnp.where(tril, c_row, neg_inf), axis=-1, keepdims=True), m_prev)
            b_col = jnp.sum(jnp.where(tril, lf_row, 0.0), axis=-1, keepdims=True)
            g_last = g_col[L - 1:L, :]
            b_last = b_col[L - 1:L, :]
            dmat = jnp.exp(jnp.where(tril, c_row - g_col, neg_inf))
            s = jnp.dot(q, kt32.astype(BF16), preferred_element_type=F32) * dmat
            inter = jnp.exp(m_prev - g_col)
            lhs = jnp.concatenate([s.astype(BF16), (q.astype(F32) * inter).astype(BF16)], axis=1)
            rhs = jnp.concatenate([v_aug, s_state.astype(BF16)], axis=0)
            out = jnp.dot(lhs, rhs, preferred_element_type=F32)
            num = out[:, :DH_M]
            den = out[:, DH_M:]
            hm = num / jnp.maximum(jnp.abs(den), jnp.exp(-(b_col + g_col)))
            hm = hm * og_ref[r0:r0 + L, h * DH_M:(h + 1) * DH_M].astype(F32)
            hm = hm * lax.rsqrt(jnp.mean(hm * hm, axis=-1, keepdims=True) + EPS)
            out_ref[r0:r0 + L, h * DH_M:(h + 1) * DH_M] = (
                hm * gmh_ref[:, h * DH_M:(h + 1) * DH_M]).astype(out_ref.dtype)
            wk_row = jnp.exp(c_row - g_last)
            decay = jnp.exp(m_prev - g_last)
            s_state = decay * s_state + jnp.dot((kt32 * wk_row).astype(BF16), v_aug,
                                                preferred_element_type=F32)
            m_prev = b_last + g_last
        s_ref[h] = s_state
        m_ref[h] = m_prev

    groups = tm // SUBLANES
    a = a_ref[...].reshape(groups, SUBLANES, D_R)
    u = u_ref[...].reshape(groups, SUBLANES, D_R)
    row8 = lax.broadcasted_iota(jnp.int32, (groups, SUBLANES, D_R), 1)
    for sft in (1, 2, 4):
        keep = row8 >= sft
        a_sh = pltpu.roll(a, sft, axis=1)
        u_sh = pltpu.roll(u, sft, axis=1)
        u = jnp.where(keep, a * u_sh + u, u)
        a = jnp.where(keep, a * a_sh, a)
    hprev = hr_ref[...]
    rows = []
    for gi in range(groups):
        blk = u[gi] + a[gi] * hprev
        rows.append(blk)
        hprev = blk[SUBLANES - 1:SUBLANES, :]
    hr_ref[...] = hprev
    hr = jnp.concatenate(rows, axis=0) * gy_ref[...].astype(F32)
    sq = hr * hr
    sq_hi = sq.astype(BF16)
    sq_lo = (sq - sq_hi.astype(F32)).astype(BF16)
    gsum = (jnp.dot(sq_hi, ones_ref[...], preferred_element_type=F32)
            + jnp.dot(sq_lo, ones_ref[...], preferred_element_type=F32))
    hr = hr * lax.rsqrt(gsum * (1.0 / DB_R) + EPS) * gr_ref[...]
    out_ref[:, D_M:] = hr.astype(out_ref.dtype)


def _cast_slices(w, steps):
    rows, cols = w.shape
    used = max(d for d in range(1, steps + 1)
               if steps % d == 0 and rows % d == 0 and (rows // d) % (2 * SUBLANES) == 0)
    per = steps // used
    return (rows // used, cols), per


def _mixer(q, k, v, og, a, u, gy, gt, gmh, gr, ones_bd, batch, seq, to_bf16=()):
    tm = ROW_TILE
    spb = seq // tm
    row = lambda b, s: (b * spb + s, 0)
    const = lambda b, s: (0, 0)
    n = batch * seq
    cast_specs = []
    for w in to_bf16:
        blk, per = _cast_slices(w, batch * spb)
        cast_specs.append(pl.BlockSpec(blk, functools.partial(
            lambda b, s, per: ((b * spb + s) // per, 0), per=per)))
    outs = pl.pallas_call(
        functools.partial(_mixer_kernel, n_cast=len(to_bf16)),
        grid=(batch, spb),
        in_specs=[
            pl.BlockSpec((tm, D_M), row),
            pl.BlockSpec((tm, D_M), row),
            pl.BlockSpec((tm, D_M), row),
            pl.BlockSpec((tm, D_M), row),
            pl.BlockSpec((tm, D_R), row),
            pl.BlockSpec((tm, D_R), row),
            pl.BlockSpec((tm, D_R), row),
            pl.BlockSpec((2 * NH_M, tm), lambda b, s: (0, b * spb + s)),
            pl.BlockSpec((1, D_M), const),
            pl.BlockSpec((1, D_R), const),
            pl.BlockSpec((D_R, D_R), const),
        ] + cast_specs,
        out_specs=[pl.BlockSpec((tm, D_MODEL), row)] + cast_specs,
        out_shape=[jax.ShapeDtypeStruct((n, D_MODEL), BF16)]
        + [jax.ShapeDtypeStruct(w.shape, BF16) for w in to_bf16],
        scratch_shapes=[
            pltpu.VMEM((NH_M, DH_M, 2 * DH_M), F32),
            pltpu.VMEM((NH_M, 1, 1), F32),
            pltpu.VMEM((1, D_R), F32),
        ],
        compiler_params=pltpu.CompilerParams(
            dimension_semantics=("arbitrary", "arbitrary"), vmem_limit_bytes=_vmem_limit(58 << 20)),
        name="seq_mixer",
    )(q, k, v, og, a, u, gy, gt, gmh, gr, ones_bd, *to_bf16)
    return outs[0], outs[1:]


def _outproj_router_kernel(h_ref, hc_ref, w_ref, g_ref, wr_ref, h1_ref, hn_ref, ri_ref, rf_ref, cnt_ref,
                           carry_ref):
    tm = h_ref.shape[0]

    @pl.when(pl.program_id(0) == 0)
    def _():
        carry_ref[...] = jnp.zeros_like(carry_ref)

    h1 = h_ref[...] + jnp.dot(hc_ref[...], w_ref[...], preferred_element_type=F32)
    h1_ref[...] = h1
    hn = _rmsnorm(h1, g_ref[...])
    _store_rows_3d(hn_ref, hn)
    lane_i = lax.broadcasted_iota(jnp.int32, (tm, LANES), 1)
    lane = lane_i.astype(F32)
    hn_hi = hn.astype(BF16)
    hn_lo = (hn - hn_hi.astype(F32)).astype(BF16)
    wr = wr_ref[...]
    wr_hi = wr.astype(BF16)
    wr_lo = (wr - wr_hi.astype(F32)).astype(BF16)
    logits = (jnp.dot(hn_hi, wr_hi, preferred_element_type=F32)
              + (jnp.dot(hn_hi, wr_lo, preferred_element_type=F32)
                 + jnp.dot(hn_lo, wr_hi, preferred_element_type=F32)))
    logits = jnp.where(lane_i < N_EXPERTS, logits, -jnp.inf)
    v1 = jnp.max(logits, axis=-1, keepdims=True)
    i1 = jnp.min(jnp.where(logits == v1, lane, float(LANES)), axis=-1, keepdims=True)
    m1 = lane == i1
    rest = jnp.where(m1, -jnp.inf, logits)
    v2 = jnp.max(rest, axis=-1, keepdims=True)
    i2 = jnp.min(jnp.where(rest == v2, lane, float(LANES)), axis=-1, keepdims=True)
    m2 = lane == i2
    e2 = jnp.exp(v2 - v1)
    g1 = 1.0 / (1.0 + e2)
    g2 = e2 / (1.0 + e2)
    carry = carry_ref[...] + jnp.sum(jnp.where(m1 | m2, 1.0, 0.0), axis=0, keepdims=True)
    carry_ref[...] = carry
    ri_ref[...] = jnp.where(lane_i == 0, i1.astype(jnp.int32), i2.astype(jnp.int32))
    rf_ref[...] = jnp.where(lane_i == 0, g1, g2)
    cnt_ref[...] = jnp.broadcast_to(carry, cnt_ref.shape)


def _outproj_router(h, hcat, w_out, g, w_router_pad):
    n = h.shape[0]
    tm = ROW_TILE
    row = lambda i: (i, 0)
    const = lambda i: (0, 0)
    return pl.pallas_call(
        _outproj_router_kernel,
        grid=(n // tm,),
        in_specs=[
            pl.BlockSpec((tm, D_MODEL), row),
            pl.BlockSpec((tm, D_MODEL), row),
            pl.BlockSpec((D_MODEL, D_MODEL), const),
            pl.BlockSpec((1, D_MODEL), const),
            pl.BlockSpec((D_MODEL, LANES), const),
        ],
        out_specs=[pl.BlockSpec((tm, D_MODEL), row),
                   pl.BlockSpec((tm, ROW_SLABS, LANES), lambda i: (i, 0, 0)),
                   pl.BlockSpec((tm, LANES), row), pl.BlockSpec((tm, LANES), row),
                   pl.BlockSpec((SUBLANES, LANES), const)],
        out_shape=[jax.ShapeDtypeStruct((n, D_MODEL), F32),
                   jax.ShapeDtypeStruct((n, ROW_SLABS, LANES), F32),
                   jax.ShapeDtypeStruct((n, LANES), jnp.int32), jax.ShapeDtypeStruct((n, LANES), F32),
                   jax.ShapeDtypeStruct((SUBLANES, LANES), F32)],
        scratch_shapes=[pltpu.VMEM((1, LANES), F32)],
        compiler_params=pltpu.CompilerParams(
            dimension_semantics=("arbitrary",), vmem_limit_bytes=_vmem_limit(40 << 20)),
        name="outproj_router",
    )(h, hcat, w_out, g, w_router_pad)


def _swiglu(x, wg_ref, wu_ref, wd_ref, act_ref):
    for f0 in range(0, D_FF, FFN_COL_CHUNK):
        g = jnp.dot(x, wg_ref[:, f0:f0 + FFN_COL_CHUNK], preferred_element_type=F32)
        u = jnp.dot(x, wu_ref[:, f0:f0 + FFN_COL_CHUNK], preferred_element_type=F32)
        act_ref[:, f0:f0 + FFN_COL_CHUNK] = (_silu(g) * u).astype(BF16)
    return jnp.dot(act_ref[...], wd_ref[...], preferred_element_type=F32)


def _dense_layer_kernel(h_ref, hc_ref, p_ref, wo_ref, gffn_ref, wg_ref, wu_ref, wd_ref, gple_ref, wpg_ref,
                        wpp_ref, gfin_ref, out_ref, act_ref, *, final_norm):
    h1 = h_ref[...] + jnp.dot(hc_ref[...], wo_ref[...], preferred_element_type=F32)
    hn = _rmsnorm(h1, gffn_ref[...]).astype(BF16)
    h2 = h1 + _swiglu(hn, wg_ref, wu_ref, wd_ref, act_ref)
    h3 = _ple_tail(h2, p_ref[...], gple_ref[...], wpg_ref, wpp_ref)
    out_ref[...] = _rmsnorm(h3, gfin_ref[...]) if final_norm else h3


def _dense_layer(h, hcat, p, w_out, g_ffn, w_gate, w_up, w_down, g_ple, w_pg, w_pp, g_final, final_norm):
    n = h.shape[0]
    tm = FFN_ROW_TILE
    row = lambda i: (i, 0)
    const = lambda i: (0, 0)
    resident = functools.partial(pl.BlockSpec, index_map=const, pipeline_mode=pl.Buffered(1))
    return pl.pallas_call(
        functools.partial(_dense_layer_kernel, final_norm=final_norm),
        grid=(n // tm,),
        in_specs=[
            pl.BlockSpec((tm, D_MODEL), row),
            pl.BlockSpec((tm, D_MODEL), row),
            pl.BlockSpec((tm, D_PLE), row),
            resident((D_MODEL, D_MODEL)),
            resident((1, D_MODEL)),
            resident((D_MODEL, D_FF)),
            resident((D_MODEL, D_FF)),
            resident((D_FF, D_MODEL)),
            resident((1, D_MODEL)),
            resident((D_MODEL, D_MODEL)),
            resident((D_PLE, D_MODEL)),
            resident((1, D_MODEL)),
        ],
        out_specs=pl.BlockSpec((tm, D_MODEL), row),
        out_shape=jax.ShapeDtypeStruct((n, D_MODEL), F32),
        scratch_shapes=[pltpu.VMEM((tm, D_FF), BF16)],
        compiler_params=pltpu.CompilerParams(
            dimension_semantics=("arbitrary",), vmem_limit_bytes=_vmem_limit(58 << 20)),
        name="dense_layer",
    )(h, hcat, p, w_out, g_ffn, w_gate, w_up, w_down, g_ple, w_pg, w_pp, g_final)


KEY_STRIDE = 1 << 16

ROW_SLABS = D_MODEL // LANES


def _rows_to_2d(ref3):
    return ref3[...].reshape(ref3.shape[0], D_MODEL)


def _store_rows_3d(ref3, val):
    ref3[...] = val.reshape(val.shape[0], ROW_SLABS, LANES)


ROW_COPY_UNROLL = 8


def _moe_ffn_kernel(te_ref, nt_ref, src_ref, src_next_ref, dst_ref, hn_ref, wg_ref, wu_ref, wd_ref,
                    out_ref, xbuf, ybuf, act_ref, sem_g, sem_s, *, n_tok):
    i = pl.program_id(0)
    nt = nt_ref[0]
    tmf = xbuf.shape[1]
    slot = i % 2
    dump0 = TOP_K * n_tok

    def gather(sref, slot_):
        def body(j, carry):
            for u in range(ROW_COPY_UNROLL):
                r = ROW_COPY_UNROLL * j + u
                pltpu.make_async_copy(hn_ref.at[sref[0, 0, r]], xbuf.at[slot_, r],
                                      sem_g.at[slot_]).start(priority=u % 2)
            return carry
        lax.fori_loop(0, tmf // ROW_COPY_UNROLL, body, 0)

    def wait_rows(sem):
        pltpu.make_async_copy(hn_ref.at[pl.ds(0, tmf)], xbuf.at[0], sem).wait()

    @pl.when(i == 0)
    def _():
        xbuf[0] = jnp.zeros(xbuf.shape[1:], xbuf.dtype)
        for s in range(2):
            fill = pltpu.make_async_copy(xbuf.at[0], out_ref.at[pl.ds(dump0 + s * tmf, tmf)], sem_s.at[s])
            fill.start()
            fill.wait()
        gather(src_ref, 0)

    @pl.when(i + 1 < nt)
    def _():
        gather(src_next_ref, 1 - slot)

    @pl.when(i < nt)
    def _():
        wait_rows(sem_g.at[slot])
        y = _swiglu(_rows_to_2d(xbuf.at[slot]).astype(BF16), wg_ref, wu_ref, wd_ref, act_ref)

        @pl.when(i >= 2)
        def _():
            wait_rows(sem_s.at[slot])

        _store_rows_3d(ybuf.at[slot], y)

        def body(j, carry):
            for u in range(ROW_COPY_UNROLL):
                r = ROW_COPY_UNROLL * j + u
                pltpu.make_async_copy(ybuf.at[slot, r], out_ref.at[dst_ref[0, 0, r]],
                                      sem_s.at[slot]).start(priority=u % 2)
            return carry
        lax.fori_loop(0, tmf // ROW_COPY_UNROLL, body, 0)

    @pl.when(i == pl.num_programs(0) - 1)
    def _():
        @pl.when(nt >= 2)
        def _():
            wait_rows(sem_s.at[nt % 2])
        wait_rows(sem_s.at[(nt - 1) % 2])


def _moe_ffn(hn, w_gate, w_up, w_down, src_tiles, dst_tiles, tile_expert, n_tiles):
    n_tok = hn.shape[0]
    tmf = FFN_ROW_TILE
    t_total = src_tiles.shape[0]
    grid_spec = pltpu.PrefetchScalarGridSpec(
        num_scalar_prefetch=2,
        grid=(t_total,),
        in_specs=[
            pl.BlockSpec((1, 1, tmf), lambda i, te, nt: (i, 0, 0), memory_space=pltpu.SMEM),
            pl.BlockSpec((1, 1, tmf), lambda i, te, nt: (jnp.minimum(i + 1, t_total - 1), 0, 0),
                         memory_space=pltpu.SMEM),
            pl.BlockSpec((1, 1, tmf), lambda i, te, nt: (i, 0, 0), memory_space=pltpu.SMEM),
            pl.BlockSpec(memory_space=pl.ANY),
            pl.BlockSpec((None, D_MODEL, D_FF), lambda i, te, nt: (te[i], 0, 0)),
            pl.BlockSpec((None, D_MODEL, D_FF), lambda i, te, nt: (te[i], 0, 0)),
            pl.BlockSpec((None, D_FF, D_MODEL), lambda i, te, nt: (te[i], 0, 0)),
        ],
        out_specs=pl.BlockSpec(memory_space=pl.ANY),
        scratch_shapes=[
            pltpu.VMEM((2, tmf, ROW_SLABS, LANES), F32),
            pltpu.VMEM((2, tmf, ROW_SLABS, LANES), F32),
            pltpu.VMEM((tmf, D_FF), BF16),
            pltpu.SemaphoreType.DMA((2,)),
            pltpu.SemaphoreType.DMA((2,)),
        ],
    )
    return pl.pallas_call(
        functools.partial(_moe_ffn_kernel, n_tok=n_tok),
        grid_spec=grid_spec,
        out_shape=jax.ShapeDtypeStruct((TOP_K * n_tok + 2 * tmf, ROW_SLABS, LANES), F32),
        compiler_params=pltpu.CompilerParams(
            dimension_semantics=("arbitrary",), vmem_limit_bytes=_vmem_limit(58 << 20),
            has_side_effects=True),
        name="expert_swiglu",
    )(tile_expert, n_tiles, src_tiles, src_tiles, dst_tiles, hn, w_gate, w_up, w_down)


def _ple_tail(h2, p_tile, g_ple, wpg_ref, wpp_ref):
    gate = _sigmoid(jnp.dot(_rmsnorm(h2, g_ple).astype(BF16), wpg_ref[...], preferred_element_type=F32))
    return h2 + gate * jnp.dot(p_tile.astype(BF16), wpp_ref[...], preferred_element_type=F32)


def _combine_kernel(h1_ref, rf_ref, y0_ref, y1_ref, p_ref, gple_ref, wpg_ref, wpp_ref, gfin_ref,
                    out_ref, *, final_norm):
    rf = rf_ref[...]
    h2 = h1_ref[...] + rf[:, 0:1] * _rows_to_2d(y0_ref) + rf[:, 1:2] * _rows_to_2d(y1_ref)
    h3 = _ple_tail(h2, p_ref[...], gple_ref[...], wpg_ref, wpp_ref)
    out_ref[...] = _rmsnorm(h3, gfin_ref[...]) if final_norm else h3


def _combine(h1, rf, ys, p, g_ple, w_pg, w_pp, g_final, final_norm):
    n = h1.shape[0]
    tm = ROW_TILE
    row = lambda i: (i, 0)
    const = lambda i: (0, 0)
    return pl.pallas_call(
        functools.partial(_combine_kernel, final_norm=final_norm),
        grid=(n // tm,),
        in_specs=[
            pl.BlockSpec((tm, D_MODEL), row),
            pl.BlockSpec((tm, LANES), row),
            pl.BlockSpec((tm, ROW_SLABS, LANES), lambda i: (i, 0, 0)),
            pl.BlockSpec((tm, ROW_SLABS, LANES), lambda i: (n // tm + i, 0, 0)),
            pl.BlockSpec((tm, D_PLE), row),
            pl.BlockSpec((1, D_MODEL), const),
            pl.BlockSpec((D_MODEL, D_MODEL), const),
            pl.BlockSpec((D_PLE, D_MODEL), const),
            pl.BlockSpec((1, D_MODEL), const),
        ],
        out_specs=pl.BlockSpec((tm, D_MODEL), row),
        out_shape=jax.ShapeDtypeStruct((n, D_MODEL), F32),
        compiler_params=pltpu.CompilerParams(
            dimension_semantics=("arbitrary",), vmem_limit_bytes=_vmem_limit(40 << 20)),
        name="expert_combine_tail",
    )(h1, rf, ys, ys, p, g_ple, w_pg, w_pp, g_final)


def _block_diag(w):
    nb, db, _ = w.shape
    eye = jnp.eye(nb, dtype=w.dtype)
    return (eye[:, None, :, None] * w[:, :, None, :]).reshape(nb * db, nb * db)


def _mixer_layer(h, g_mix, w_in, b_in, w_conv_qk, b_conv_qk, g_mh, w_conv_r, b_conv_r,
                 w_ra, b_ra, w_ri, b_ri, lam, g_r, batch, seq, to_bf16):
    q_end, k_end, v_end, o_end = D_M, 2 * D_M, 3 * D_M, 4 * D_M
    i_end = o_end + NH_M
    f_end = i_end + NH_M
    xr_end = f_end + D_R
    w_main = jnp.concatenate([w_in[:, :o_end], w_in[:, f_end:]], axis=1).astype(BF16)
    b_main = jnp.concatenate([b_in[:o_end], b_in[f_end:]])[None, :]
    w_gt = w_in[:, o_end:f_end].T.astype(BF16)
    b_gt = b_in[o_end:f_end][:, None]
    del q_end, k_end, v_end, xr_end
    wbd = jnp.concatenate([_block_diag(w_ra), _block_diag(w_ri)], axis=1).astype(BF16)
    bbd = jnp.concatenate([b_ra, b_ri])[None, :]
    ones_bd = _block_diag(jnp.ones((NB_R, DB_R, DB_R), F32)).astype(BF16)
    q, k, v, og, a, u, gy, gt = _norm_inproj(
        h, g_mix[None, :], w_main, b_main, w_gt, b_gt, w_conv_qk, b_conv_qk[None, :],
        w_conv_r, b_conv_r[None, :], wbd, bbd, lam[None, :], seq)
    return _mixer(q, k, v, og, a, u, gy, gt, g_mh[None, :], g_r[None, :], ones_bd, batch, seq, to_bf16)


def _expert_tiles(top_idx, counts_f, n_tok):
    tmf = FFN_ROW_TILE
    counts = counts_f.astype(jnp.int32)
    padded = (counts + tmf - 1) // tmf * tmf
    n_tiles = (jnp.sum(padded) // tmf).astype(jnp.int32).reshape(1)
    a_ids = jnp.arange(TOP_K * n_tok, dtype=jnp.int32)
    real = top_idx.reshape(-1) * KEY_STRIDE + a_ids
    pad_j = jnp.arange(tmf, dtype=jnp.int32)[None, :]
    experts = jnp.arange(N_EXPERTS, dtype=jnp.int32)[:, None]
    pad_keys = jnp.where(pad_j < (padded - counts)[:, None],
                         experts * KEY_STRIDE + TOP_K * n_tok + pad_j, N_EXPERTS * KEY_STRIDE)
    keys = jnp.sort(jnp.concatenate([real, pad_keys.reshape(-1)]))
    t_total = (TOP_K * n_tok) // tmf + N_EXPERTS
    a_sorted = keys & (KEY_STRIDE - 1)
    valid = a_sorted < TOP_K * n_tok
    row = jnp.arange(t_total * tmf, dtype=jnp.int32)
    dump = TOP_K * n_tok + (row // tmf % 2) * tmf + row % tmf
    src = jnp.where(valid, a_sorted >> 1, 0).reshape(t_total, 1, tmf)
    dst = jnp.where(valid, (a_sorted & 1) * n_tok + (a_sorted >> 1), dump).reshape(t_total, 1, tmf)
    tile_expert = jnp.minimum(keys[::tmf] // KEY_STRIDE, N_EXPERTS - 1).astype(jnp.int32)
    return src, dst, tile_expert, n_tiles


def kernel(x, p, g_mix, w_in, b_in, w_conv_qk, b_conv_qk, g_mh, w_conv_r, b_conv_r, w_ra, b_ra, w_ri, b_ri, lam, g_r, w_out, g_ffn, w_ff_gate, w_ff_up, w_ff_down, w_router, w_e_gate, w_e_up, w_e_down, g_ple, w_ple_gate, w_ple_proj, g_final):
    batch, seq, _ = x.shape
    n = batch * seq
    depth = g_mix.shape[0]
    h = x.reshape(n, D_MODEL)
    tmf = FFN_ROW_TILE
    for i in range(depth):
        last = i == depth - 1
        j = i // 2
        dense = i % 2 == 0
        if dense:
            ffn_w = [w_ff_gate[j], w_ff_up[j], w_ff_down[j]]
        else:
            ffn_w = [w_e_gate[j].reshape(N_EXPERTS * D_MODEL, D_FF), w_e_up[j].reshape(N_EXPERTS * D_MODEL, D_FF),
                     w_e_down[j].reshape(N_EXPERTS * D_FF, D_MODEL)]
        hcat, (w_o, w_pg, w_pp, w_g, w_u, w_d) = _mixer_layer(
            h, g_mix[i], w_in[i], b_in[i], w_conv_qk[i], b_conv_qk[i], g_mh[i], w_conv_r[i], b_conv_r[i],
            w_ra[i], b_ra[i], w_ri[i], b_ri[i], lam[i], g_r[i], batch, seq,
            [w_out[i], w_ple_gate[i], w_ple_proj[i]] + ffn_w)
        p_i = p[i].reshape(n, D_PLE)
        if dense:
            h = _dense_layer(h, hcat, p_i, w_o, g_ffn[i][None, :], w_g, w_u, w_d, g_ple[i][None, :],
                             w_pg, w_pp, g_final[None, :], last)
        else:
            w_r = jnp.zeros((D_MODEL, LANES), F32).at[:, :N_EXPERTS].set(w_router[j])
            h1, hn, ri, rf, cnt = _outproj_router(h, hcat, w_o, g_ffn[i][None, :], w_r)
            src, dst, tile_expert, n_tiles = _expert_tiles(ri[:, :TOP_K], cnt[0, :N_EXPERTS], n)
            ys = _moe_ffn(hn, w_g.reshape(N_EXPERTS, D_MODEL, D_FF), w_u.reshape(N_EXPERTS, D_MODEL, D_FF),
                          w_d.reshape(N_EXPERTS, D_FF, D_MODEL), src, dst, tile_expert, n_tiles)
            h = _combine(h1, rf, ys, p_i, g_ple[i][None, :], w_pg, w_pp, g_final[None, :], last)
    return h.reshape(batch, seq, D_MODEL)
```

```python
import functools

import jax
import jax.numpy as jnp
from jax import lax
from jax.experimental import pallas as pl
from jax.experimental.pallas import tpu as pltpu

F32 = jnp.float32
BF16 = jnp.bfloat16

D_MODEL = 1024
NH_M = 4
DH_M = 128
D_M = NH_M * DH_M
D_R = D_MODEL - D_M
NB_R = 8
DB_R = D_R // NB_R
CONV_W = 4
LRU_C = 8.0
D_FF = 2816
N_EXPERTS = 8
TOP_K = 2
D_PLE = 256
EPS = 1e-6

LANES = 128
SUBLANES = 8
V7X_VMEM_BYTES = 64 * 1024 * 1024

ROW_TILE = 512
MIX_CHUNK = 128
FFN_ROW_TILE = 512
FFN_COL_CHUNK = 256
INPROJ_COL_CHUNK = 256
INPROJ_ROW_TILE = 512


def _vmem_limit(nbytes):
    return int(min(nbytes, V7X_VMEM_BYTES - 6 * 1024 * 1024))


def _rmsnorm(x, g):
    return x * lax.rsqrt(jnp.mean(x * x, axis=-1, keepdims=True) + EPS) * g


def _sigmoid(x):
    return 1.0 / (1.0 + jnp.exp(-x))


def _silu(x):
    return x * _sigmoid(x)


def _causal_conv(x, prev, w, b, stage_ref):
    t = x.shape[0]
    stage_ref[0:SUBLANES, :] = prev
    stage_ref[SUBLANES:, :] = x
    acc = x * w[CONV_W - 1:CONV_W, :] + b
    for j in range(CONV_W - 1):
        lo = SUBLANES - (CONV_W - 1) + j
        acc = acc + stage_ref[lo:lo + t, :] * w[j:j + 1, :]
    return acc


def _inproj_kernel(x_ref, g_ref, w_ref, b_ref, wg_ref, bg_ref, wcqk_ref, bcqk_ref, wcr_ref, bcr_ref,
                   wbd_ref, bbd_ref, lam_ref,
                   q_ref, k_ref, v_ref, og_ref, a_ref, u_ref, gy_ref, gt_ref,
                   pqk_ref, pxr_ref, xr_ref, cstage_ref, *, tiles_per_seq):
    tm = x_ref.shape[0]

    @pl.when(pl.program_id(0) % tiles_per_seq == 0)
    def _():
        pqk_ref[...] = jnp.zeros_like(pqk_ref)
        pxr_ref[...] = jnp.zeros_like(pxr_ref)

    xn = _rmsnorm(x_ref[...], g_ref[...]).astype(BF16)
    gt_ref[...] = lax.dot_general(wg_ref[...], xn, (((1,), (1,)), ((), ())),
                                  preferred_element_type=F32) + bg_ref[...]
    cw = INPROJ_COL_CHUNK

    n_stage = [0]

    def next_stage():
        n_stage[0] += 1
        return cstage_ref.at[n_stage[0] % 2]

    def post_qk(z, c0):
        cols = slice(c0, c0 + cw)
        y = _silu(_causal_conv(z, pqk_ref[:, cols], wcqk_ref[:, cols], bcqk_ref[:, cols], next_stage()))
        pqk_ref[:, cols] = z[tm - SUBLANES:, :]
        if c0 < D_M:
            q_ref[:, cols] = (y * (DH_M ** -0.5)).astype(BF16)
        else:
            k_ref[:, c0 - D_M:c0 - D_M + cw] = y.astype(BF16)

    def post_v(z, c0):
        v_ref[:, c0:c0 + cw] = z.astype(BF16)

    def post_og(z, c0):
        og_ref[:, c0:c0 + cw] = _sigmoid(z).astype(BF16)

    def post_xr(z, c0):
        cols = slice(c0, c0 + cw)
        xr_ref[:, cols] = _causal_conv(z, pxr_ref[:, cols], wcr_ref[:, cols], bcr_ref[:, cols],
                                       next_stage())
        pxr_ref[:, cols] = z[tm - SUBLANES:, :]

    def post_gy(z, c0):
        gy_ref[:, c0:c0 + cw] = (
            0.5 * z * (1.0 + jnp.tanh(0.7978845608028654 * (z + 0.044715 * z * z * z)))).astype(BF16)

    nlam = -lam_ref[...]
    softplus = jnp.maximum(nlam, 0.0) + jnp.log(1.0 + jnp.exp(-jnp.abs(nlam)))

    def project(col):
        return lambda: (jnp.dot(xn, w_ref[:, col:col + cw], preferred_element_type=F32)
                        + b_ref[:, col:col + cw])

    def gate_matmuls(c0):
        def run():
            xr_bf = xr_ref[...].astype(BF16)
            return tuple(jnp.dot(xr_bf, wbd_ref[:, g0 + c0:g0 + c0 + cw], preferred_element_type=F32)
                         + bbd_ref[:, g0 + c0:g0 + c0 + cw] for g0 in (0, D_R))
        return run

    def post_gates(z, c0):
        cols = slice(c0, c0 + cw)
        log_a = (-LRU_C) * _sigmoid(z[0]) * softplus[:, cols]
        a_ref[:, cols] = jnp.exp(log_a)
        u_ref[:, cols] = jnp.sqrt(1.0 - jnp.exp(2.0 * log_a)) * (_sigmoid(z[1]) * xr_ref[:, cols])

    stages = []
    for base, width, post in ((0, 2 * D_M, post_qk), (2 * D_M, D_M, post_v), (3 * D_M, D_M, post_og),
                              (4 * D_M, D_R, post_xr), (4 * D_M + D_R, D_R, post_gy)):
        stages += [(project(base + c0), c0, post) for c0 in range(0, width, cw)]
    stages += [(gate_matmuls(c0), c0, post_gates) for c0 in range(0, D_R, cw)]

    z_next = stages[0][0]()
    for idx, (_, c0, post) in enumerate(stages):
        z = z_next
        if idx + 1 < len(stages):
            z_next = stages[idx + 1][0]()
        post(z, c0)


def _norm_inproj(h, g, w_main, b_main, w_gt, b_gt, wcqk, bcqk, wcr, bcr, wbd, bbd, lam, seq):
    n = h.shape[0]
    tm = INPROJ_ROW_TILE
    wcols = w_main.shape[1]
    row = lambda i: (i, 0)
    const = lambda i: (0, 0)
    return pl.pallas_call(
        functools.partial(_inproj_kernel, tiles_per_seq=seq // tm),
        grid=(n // tm,),
        in_specs=[
            pl.BlockSpec((tm, D_MODEL), row),
            pl.BlockSpec((1, D_MODEL), const),
            pl.BlockSpec((D_MODEL, wcols), const),
            pl.BlockSpec((1, wcols), const),
            pl.BlockSpec((2 * NH_M, D_MODEL), const),
            pl.BlockSpec((2 * NH_M, 1), const),
            pl.BlockSpec((CONV_W, 2 * D_M), const),
            pl.BlockSpec((1, 2 * D_M), const),
            pl.BlockSpec((CONV_W, D_R), const),
            pl.BlockSpec((1, D_R), const),
            pl.BlockSpec((D_R, 2 * D_R), const),
            pl.BlockSpec((1, 2 * D_R), const),
            pl.BlockSpec((1, D_R), const),
        ],
        out_specs=[
            pl.BlockSpec((tm, D_M), row),
            pl.BlockSpec((tm, D_M), row),
            pl.BlockSpec((tm, D_M), row),
            pl.BlockSpec((tm, D_M), row),
            pl.BlockSpec((tm, D_R), row),
            pl.BlockSpec((tm, D_R), row),
            pl.BlockSpec((tm, D_R), row),
            pl.BlockSpec((2 * NH_M, tm), lambda i: (0, i)),
        ],
        out_shape=[
            jax.ShapeDtypeStruct((n, D_M), BF16),
            jax.ShapeDtypeStruct((n, D_M), BF16),
            jax.ShapeDtypeStruct((n, D_M), BF16),
            jax.ShapeDtypeStruct((n, D_M), BF16),
            jax.ShapeDtypeStruct((n, D_R), F32),
            jax.ShapeDtypeStruct((n, D_R), F32),
            jax.ShapeDtypeStruct((n, D_R), BF16),
            jax.ShapeDtypeStruct((2 * NH_M, n), F32),
        ],
        scratch_shapes=[pltpu.VMEM((SUBLANES, 2 * D_M), F32), pltpu.VMEM((SUBLANES, D_R), F32),
                        pltpu.VMEM((tm, D_R), F32),
                        pltpu.VMEM((2, tm + SUBLANES, INPROJ_COL_CHUNK), F32)],
        compiler_params=pltpu.CompilerParams(
            dimension_semantics=("arbitrary",), vmem_limit_bytes=_vmem_limit(48 << 20)),
        name="norm_inproj",
    )(h, g, w_main, b_main, w_gt, b_gt, wcqk, bcqk, wcr, bcr, wbd, bbd, lam)


def _mixer_kernel(q_ref, k_ref, v_ref, og_ref, a_ref, u_ref, gy_ref, gt_ref, gmh_ref, gr_ref, ones_ref,
                  *rest, n_cast):
    cast_in, (out_ref,), cast_out = rest[:n_cast], rest[n_cast:n_cast + 1], rest[n_cast + 1:2 * n_cast + 1]
    s_ref, m_ref, hr_ref = rest[2 * n_cast + 1:]
    tm = q_ref.shape[0]
    for src, dst in zip(cast_in, cast_out):
        dst[...] = src[...].astype(dst.dtype)

    @pl.when(pl.program_id(1) == 0)
    def _():
        s_ref[...] = jnp.zeros_like(s_ref)
        m_ref[...] = jnp.zeros_like(m_ref)
        hr_ref[...] = jnp.zeros_like(hr_ref)

    gt = gt_ref[...]
    logf = jnp.minimum(gt, 0.0) - jnp.log(1.0 + jnp.exp(-jnp.abs(gt)))
    L = MIX_CHUNK
    ri = lax.broadcasted_iota(jnp.int32, (L, L), 0)
    ci = lax.broadcasted_iota(jnp.int32, (L, L), 1)
    tril = ci <= ri
    upper = (ri <= ci).astype(F32)
    ones_blk = jnp.ones((L, DH_M), BF16)
    neg_inf = -jnp.inf

    b_all = jnp.concatenate(
        [jnp.dot(logf[:, c * L:(c + 1) * L], upper, preferred_element_type=F32,
                 precision=lax.Precision.HIGHEST) for c in range(tm // L)], axis=1)

    for h in range(NH_M):
        s_state = s_ref[h]
        m_prev = m_ref[h]
        for c in range(tm // L):
            r0 = c * L
            q = q_ref[r0:r0 + L, h * DH_M:(h + 1) * DH_M]
            kt32 = k_ref[r0:r0 + L, h * DH_M:(h + 1) * DH_M].astype(F32).T
            v_aug = jnp.concatenate([v_ref[r0:r0 + L, h * DH_M:(h + 1) * DH_M], ones_blk], axis=1)
            ig_row = gt[h:h + 1, r0:r0 + L]
            lf_row = logf[NH_M + h:NH_M + h + 1, r0:r0 + L]
            b_row = b_all[NH_M + h:NH_M + h + 1, r0:r0 + L]
            c_row = ig_row - b_row
            g_col = jnp.maximum(jnp.max(jnp.where(tril, c_row, neg_inf), axis=-1, keepdims=True), m_prev)
            b_col = jnp.sum(jnp.where(tril, lf_row, 0.0), axis=-1, keepdims=True)
            g_last = g_col[L - 1:L, :]
            b_last = b_col[L - 1:L, :]
            dmat = jnp.exp(jnp.where(tril, c_row - g_col, neg_inf))
            s = jnp.dot(q, kt32.astype(BF16), preferred_element_type=F32) * dmat
            inter = jnp.exp(m_prev - g_col)
            lhs = jnp.concatenate([s.astype(BF16), (q.astype(F32) * inter).astype(BF16)], axis=1)
            rhs = jnp.concatenate([v_aug, s_state.astype(BF16)], axis=0)
            out = jnp.dot(lhs, rhs, preferred_element_type=F32)
            num = out[:, :DH_M]
            den = out[:, DH_M:]
            hm = num / jnp.maximum(jnp.abs(den), jnp.exp(-(b_col + g_col)))
            hm = hm * og_ref[r0:r0 + L, h * DH_M:(h + 1) * DH_M].astype(F32)
            hm = hm * lax.rsqrt(jnp.mean(hm * hm, axis=-1, keepdims=True) + EPS)
            out_ref[r0:r0 + L, h * DH_M:(h + 1) * DH_M] = (
                hm * gmh_ref[:, h * DH_M:(h + 1) * DH_M]).astype(out_ref.dtype)
            wk_row = jnp.exp(c_row - g_last)
            decay = jnp.exp(m_prev - g_last)
            s_state = decay * s_state + jnp.dot((kt32 * wk_row).astype(BF16), v_aug,
                                                preferred_element_type=F32)
            m_prev = b_last + g_last
        s_ref[h] = s_state
        m_ref[h] = m_prev

    groups = tm // SUBLANES
    a = a_ref[...].reshape(groups, SUBLANES, D_R)
    u = u_ref[...].reshape(groups, SUBLANES, D_R)
    row8 = lax.broadcasted_iota(jnp.int32, (groups, SUBLANES, D_R), 1)
    for sft in (1, 2, 4):
        keep = row8 >= sft
        a_sh = pltpu.roll(a, sft, axis=1)
        u_sh = pltpu.roll(u, sft, axis=1)
        u = jnp.where(keep, a * u_sh + u, u)
        a = jnp.where(keep, a * a_sh, a)
    hprev = hr_ref[...]
    rows = []
    for gi in range(groups):
        blk = u[gi] + a[gi] * hprev
        rows.append(blk)
        hprev = blk[SUBLANES - 1:SUBLANES, :]
    hr_ref[...] = hprev
    hr = jnp.concatenate(rows, axis=0) * gy_ref[...].astype(F32)
    sq = hr * hr
    sq_hi = sq.astype(BF16)
    sq_lo = (sq - sq_hi.astype(F32)).astype(BF16)
    gsum = (jnp.dot(sq_hi, ones_ref[...], preferred_element_type=F32)
            + jnp.dot(sq_lo, ones_ref[...], preferred_element_type=F32))
    hr = hr * lax.rsqrt(gsum * (1.0 / DB_R) + EPS) * gr_ref[...]
    out_ref[:, D_M:] = hr.astype(out_ref.dtype)


def _cast_slices(w, steps):
    rows, cols = w.shape
    used = max(d for d in range(1, steps + 1)
               if steps % d == 0 and rows % d == 0 and (rows // d) % (2 * SUBLANES) == 0)
    per = steps // used
    return (rows // used, cols), per


def _mixer(q, k, v, og, a, u, gy, gt, gmh, gr, ones_bd, batch, seq, to_bf16=()):
    tm = ROW_TILE
    spb = seq // tm
    row = lambda b, s: (b * spb + s, 0)
    const = lambda b, s: (0, 0)
    n = batch * seq
    cast_specs = []
    for w in to_bf16:
        blk, per = _cast_slices(w, batch * spb)
        cast_specs.append(pl.BlockSpec(blk, functools.partial(
            lambda b, s, per: ((b * spb + s) // per, 0), per=per)))
    outs = pl.pallas_call(
        functools.partial(_mixer_kernel, n_cast=len(to_bf16)),
        grid=(batch, spb),
        in_specs=[
            pl.BlockSpec((tm, D_M), row),
            pl.BlockSpec((tm, D_M), row),
            pl.BlockSpec((tm, D_M), row),
            pl.BlockSpec((tm, D_M), row),
            pl.BlockSpec((tm, D_R), row),
            pl.BlockSpec((tm, D_R), row),
            pl.BlockSpec((tm, D_R), row),
            pl.BlockSpec((2 * NH_M, tm), lambda b, s: (0, b * spb + s)),
            pl.BlockSpec((1, D_M), const),
            pl.BlockSpec((1, D_R), const),
            pl.BlockSpec((D_R, D_R), const),
        ] + cast_specs,
        out_specs=[pl.BlockSpec((tm, D_MODEL), row)] + cast_specs,
        out_shape=[jax.ShapeDtypeStruct((n, D_MODEL), BF16)]
        + [jax.ShapeDtypeStruct(w.shape, BF16) for w in to_bf16],
        scratch_shapes=[
            pltpu.VMEM((NH_M, DH_M, 2 * DH_M), F32),
            pltpu.VMEM((NH_M, 1, 1), F32),
            pltpu.VMEM((1, D_R), F32),
        ],
        compiler_params=pltpu.CompilerParams(
            dimension_semantics=("arbitrary", "arbitrary"), vmem_limit_bytes=_vmem_limit(58 << 20)),
        name="seq_mixer",
    )(q, k, v, og, a, u, gy, gt, gmh, gr, ones_bd, *to_bf16)
    return outs[0], outs[1:]


def _outproj_router_kernel(h_ref, hc_ref, w_ref, g_ref, wr_ref, h1_ref, hn_ref, ri_ref, rf_ref, cnt_ref,
                           carry_ref):
    tm = h_ref.shape[0]

    @pl.when(pl.program_id(0) == 0)
    def _():
        carry_ref[...] = jnp.zeros_like(carry_ref)

    h1 = h_ref[...] + jnp.dot(hc_ref[...], w_ref[...], preferred_element_type=F32)
    h1_ref[...] = h1
    hn = _rmsnorm(h1, g_ref[...])
    _store_rows_3d(hn_ref, hn)
    lane_i = lax.broadcasted_iota(jnp.int32, (tm, LANES), 1)
    lane = lane_i.astype(F32)
    hn_hi = hn.astype(BF16)
    hn_lo = (hn - hn_hi.astype(F32)).astype(BF16)
    wr = wr_ref[...]
    wr_hi = wr.astype(BF16)
    wr_lo = (wr - wr_hi.astype(F32)).astype(BF16)
    logits = (jnp.dot(hn_hi, wr_hi, preferred_element_type=F32)
              + (jnp.dot(hn_hi, wr_lo, preferred_element_type=F32)
                 + jnp.dot(hn_lo, wr_hi, preferred_element_type=F32)))
    logits = jnp.where(lane_i < N_EXPERTS, logits, -jnp.inf)
    v1 = jnp.max(logits, axis=-1, keepdims=True)
    i1 = jnp.min(jnp.where(logits == v1, lane, float(LANES)), axis=-1, keepdims=True)
    m1 = lane == i1
    rest = jnp.where(m1, -jnp.inf, logits)
    v2 = jnp.max(rest, axis=-1, keepdims=True)
    i2 = jnp.min(jnp.where(rest == v2, lane, float(LANES)), axis=-1, keepdims=True)
    m2 = lane == i2
    e2 = jnp.exp(v2 - v1)
    g1 = 1.0 / (1.0 + e2)
    g2 = e2 / (1.0 + e2)
    carry = carry_ref[...] + jnp.sum(jnp.where(m1 | m2, 1.0, 0.0), axis=0, keepdims=True)
    carry_ref[...] = carry
    ri_ref[...] = jnp.where(lane_i == 0, i1.astype(jnp.int32), i2.astype(jnp.int32))
    rf_ref[...] = jnp.where(lane_i == 0, g1, g2)
    cnt_ref[...] = jnp.broadcast_to(carry, cnt_ref.shape)


def _outproj_router(h, hcat, w_out, g, w_router_pad):
    n = h.shape[0]
    tm = ROW_TILE
    row = lambda i: (i, 0)
    const = lambda i: (0, 0)
    return pl.pallas_call(
        _outproj_router_kernel,
        grid=(n // tm,),
        in_specs=[
            pl.BlockSpec((tm, D_MODEL), row),
            pl.BlockSpec((tm, D_MODEL), row),
            pl.BlockSpec((D_MODEL, D_MODEL), const),
            pl.BlockSpec((1, D_MODEL), const),
            pl.BlockSpec((D_MODEL, LANES), const),
        ],
        out_specs=[pl.BlockSpec((tm, D_MODEL), row),
                   pl.BlockSpec((tm, ROW_SLABS, LANES), lambda i: (i, 0, 0)),
                   pl.BlockSpec((tm, LANES), row), pl.BlockSpec((tm, LANES), row),
                   pl.BlockSpec((SUBLANES, LANES), const)],
        out_shape=[jax.ShapeDtypeStruct((n, D_MODEL), F32),
                   jax.ShapeDtypeStruct((n, ROW_SLABS, LANES), F32),
                   jax.ShapeDtypeStruct((n, LANES), jnp.int32), jax.ShapeDtypeStruct((n, LANES), F32),
                   jax.ShapeDtypeStruct((SUBLANES, LANES), F32)],
        scratch_shapes=[pltpu.VMEM((1, LANES), F32)],
        compiler_params=pltpu.CompilerParams(
            dimension_semantics=("arbitrary",), vmem_limit_bytes=_vmem_limit(40 << 20)),
        name="outproj_router",
    )(h, hcat, w_out, g, w_router_pad)


def _swiglu(x, wg_ref, wu_ref, wd_ref, act_ref, between=None):
    n_chunks = D_FF // FFN_COL_CHUNK
    for c in range(n_chunks):
        f0 = c * FFN_COL_CHUNK
        g = jnp.dot(x, wg_ref[:, f0:f0 + FFN_COL_CHUNK], preferred_element_type=F32)
        u = jnp.dot(x, wu_ref[:, f0:f0 + FFN_COL_CHUNK], preferred_element_type=F32)
        act_ref[:, f0:f0 + FFN_COL_CHUNK] = (_silu(g) * u).astype(BF16)
        if between is not None:
            between(c, n_chunks)
    return jnp.dot(act_ref[...], wd_ref[...], preferred_element_type=F32)


def _dense_layer_kernel(h_ref, hc_ref, p_ref, wo_ref, gffn_ref, wg_ref, wu_ref, wd_ref, gple_ref, wpg_ref,
                        wpp_ref, gfin_ref, out_ref, act_ref, *, final_norm):
    h1 = h_ref[...] + jnp.dot(hc_ref[...], wo_ref[...], preferred_element_type=F32)
    hn = _rmsnorm(h1, gffn_ref[...]).astype(BF16)
    h2 = h1 + _swiglu(hn, wg_ref, wu_ref, wd_ref, act_ref)
    h3 = _ple_tail(h2, p_ref[...], gple_ref[...], wpg_ref, wpp_ref)
    out_ref[...] = _rmsnorm(h3, gfin_ref[...]) if final_norm else h3


def _dense_layer(h, hcat, p, w_out, g_ffn, w_gate, w_up, w_down, g_ple, w_pg, w_pp, g_final, final_norm):
    n = h.shape[0]
    tm = FFN_ROW_TILE
    row = lambda i: (i, 0)
    const = lambda i: (0, 0)
    resident = functools.partial(pl.BlockSpec, index_map=const, pipeline_mode=pl.Buffered(1))
    return pl.pallas_call(
        functools.partial(_dense_layer_kernel, final_norm=final_norm),
        grid=(n // tm,),
        in_specs=[
            pl.BlockSpec((tm, D_MODEL), row),
            pl.BlockSpec((tm, D_MODEL), row),
            pl.BlockSpec((tm, D_PLE), row),
            resident((D_MODEL, D_MODEL)),
            resident((1, D_MODEL)),
            resident((D_MODEL, D_FF)),
            resident((D_MODEL, D_FF)),
            resident((D_FF, D_MODEL)),
            resident((1, D_MODEL)),
            resident((D_MODEL, D_MODEL)),
            resident((D_PLE, D_MODEL)),
            resident((1, D_MODEL)),
        ],
        out_specs=pl.BlockSpec((tm, D_MODEL), row),
        out_shape=jax.ShapeDtypeStruct((n, D_MODEL), F32),
        scratch_shapes=[pltpu.VMEM((tm, D_FF), BF16)],
        compiler_params=pltpu.CompilerParams(
            dimension_semantics=("arbitrary",), vmem_limit_bytes=_vmem_limit(58 << 20)),
        name="dense_layer",
    )(h, hcat, p, w_out, g_ffn, w_gate, w_up, w_down, g_ple, w_pg, w_pp, g_final)


KEY_STRIDE = 1 << 16

ROW_SLABS = D_MODEL // LANES


def _rows_to_2d(ref3):
    return ref3[...].reshape(ref3.shape[0], D_MODEL)


def _store_rows_3d(ref3, val):
    ref3[...] = val.reshape(val.shape[0], ROW_SLABS, LANES)


ROW_COPY_UNROLL = 8


N_DUMP_AREAS = 3


def _moe_ffn_kernel(te_ref, nt_ref, src_ref, src_next_ref, dst_prev_ref, dst_ref, hn_ref, wg_ref, wu_ref,
                    wd_ref, out_ref, xbuf, ybuf, act_ref, sem_g, sem_s, *, n_tok):
    i = pl.program_id(0)
    nt = nt_ref[0]
    last = pl.num_programs(0) - 1
    tmf = xbuf.shape[1]
    slot = i % 2
    dump0 = TOP_K * n_tok

    def issue_loop(copy_row):
        def body(j, carry):
            for u in range(ROW_COPY_UNROLL):
                copy_row(ROW_COPY_UNROLL * j + u, u % 2)
            return carry
        lax.fori_loop(0, tmf // ROW_COPY_UNROLL, body, 0)

    def gather_row(sref, slot_):
        return lambda r, prio: pltpu.make_async_copy(
            hn_ref.at[sref[0, 0, r]], xbuf.at[slot_, r], sem_g.at[slot_]).start(priority=prio)

    def scatter_row(dref, slot_):
        return lambda r, prio: pltpu.make_async_copy(
            ybuf.at[slot_, r], out_ref.at[dref[0, 0, r]], sem_s.at[slot_]).start(priority=prio)

    def wait_rows(sem):
        pltpu.make_async_copy(hn_ref.at[pl.ds(0, tmf)], xbuf.at[0], sem).wait()

    @pl.when(i == 0)
    def _():
        ybuf[...] = jnp.zeros_like(ybuf)
        for s in range(N_DUMP_AREAS):
            fill = pltpu.make_async_copy(ybuf.at[0], out_ref.at[pl.ds(dump0 + s * tmf, tmf)], sem_s.at[0])
            fill.start()
            fill.wait()
        issue_loop(gather_row(src_ref, 0))

    def side_copies(c, n_chunks):
        per = pl.cdiv(tmf, n_chunks)
        gather_next = gather_row(src_next_ref, 1 - slot)
        scatter_prev = scatter_row(dst_prev_ref, 1 - slot)
        for r in range(c * per, min((c + 1) * per, tmf)):
            gather_next(r, r % 2)
            scatter_prev(r, (r + 1) % 2)

    @pl.when(i < nt)
    def _():
        wait_rows(sem_g.at[slot])
        y = _swiglu(_rows_to_2d(xbuf.at[slot]).astype(BF16), wg_ref, wu_ref, wd_ref, act_ref,
                    between=side_copies)

        @pl.when(i >= 1)
        def _():
            wait_rows(sem_s.at[slot])

        _store_rows_3d(ybuf.at[slot], y)

    @pl.when(i == nt)
    def _():
        issue_loop(scatter_row(dst_prev_ref, 1 - slot))

    @pl.when((i == last) & (nt == last + 1))
    def _():
        issue_loop(scatter_row(dst_ref, slot))

    @pl.when(i == last)
    def _():
        wait_rows(sem_s.at[nt % 2])
        wait_rows(sem_s.at[(nt - 1) % 2])
        wait_rows(sem_g.at[nt % 2])


def _moe_ffn(hn, w_gate, w_up, w_down, src_tiles, dst_tiles, tile_expert, n_tiles):
    n_tok = hn.shape[0]
    tmf = FFN_ROW_TILE
    t_total = src_tiles.shape[0]
    grid_spec = pltpu.PrefetchScalarGridSpec(
        num_scalar_prefetch=2,
        grid=(t_total,),
        in_specs=[
            pl.BlockSpec((1, 1, tmf), lambda i, te, nt: (i, 0, 0), memory_space=pltpu.SMEM),
            pl.BlockSpec((1, 1, tmf), lambda i, te, nt: (jnp.minimum(i + 1, t_total - 1), 0, 0),
                         memory_space=pltpu.SMEM),
            pl.BlockSpec((1, 1, tmf), lambda i, te, nt: (jnp.where(i == 0, t_total, i - 1), 0, 0),
                         memory_space=pltpu.SMEM),
            pl.BlockSpec((1, 1, tmf), lambda i, te, nt: (i, 0, 0), memory_space=pltpu.SMEM),
            pl.BlockSpec(memory_space=pl.ANY),
            pl.BlockSpec((None, D_MODEL, D_FF), lambda i, te, nt: (te[i], 0, 0)),
            pl.BlockSpec((None, D_MODEL, D_FF), lambda i, te, nt: (te[i], 0, 0)),
            pl.BlockSpec((None, D_FF, D_MODEL), lambda i, te, nt: (te[i], 0, 0)),
        ],
        out_specs=pl.BlockSpec(memory_space=pl.ANY),
        scratch_shapes=[
            pltpu.VMEM((2, tmf, ROW_SLABS, LANES), F32),
            pltpu.VMEM((2, tmf, ROW_SLABS, LANES), F32),
            pltpu.VMEM((tmf, D_FF), BF16),
            pltpu.SemaphoreType.DMA((2,)),
            pltpu.SemaphoreType.DMA((2,)),
        ],
    )
    return pl.pallas_call(
        functools.partial(_moe_ffn_kernel, n_tok=n_tok),
        grid_spec=grid_spec,
        out_shape=jax.ShapeDtypeStruct((TOP_K * n_tok + N_DUMP_AREAS * tmf, ROW_SLABS, LANES), F32),
        compiler_params=pltpu.CompilerParams(
            dimension_semantics=("arbitrary",), vmem_limit_bytes=_vmem_limit(58 << 20),
            has_side_effects=True),
        name="expert_swiglu",
    )(tile_expert, n_tiles, src_tiles, src_tiles, dst_tiles, dst_tiles, hn, w_gate, w_up, w_down)


def _ple_tail(h2, p_tile, g_ple, wpg_ref, wpp_ref):
    gate = _sigmoid(jnp.dot(_rmsnorm(h2, g_ple).astype(BF16), wpg_ref[...], preferred_element_type=F32))
    return h2 + gate * jnp.dot(p_tile.astype(BF16), wpp_ref[...], preferred_element_type=F32)


def _combine_kernel(h1_ref, rf_ref, y0_ref, y1_ref, p_ref, gple_ref, wpg_ref, wpp_ref, gfin_ref,
                    out_ref, *, final_norm):
    rf = rf_ref[...]
    h2 = h1_ref[...] + rf[:, 0:1] * _rows_to_2d(y0_ref) + rf[:, 1:2] * _rows_to_2d(y1_ref)
    h3 = _ple_tail(h2, p_ref[...], gple_ref[...], wpg_ref, wpp_ref)
    out_ref[...] = _rmsnorm(h3, gfin_ref[...]) if final_norm else h3


def _combine(h1, rf, ys, p, g_ple, w_pg, w_pp, g_final, final_norm):
    n = h1.shape[0]
    tm = ROW_TILE
    row = lambda i: (i, 0)
    const = lambda i: (0, 0)
    return pl.pallas_call(
        functools.partial(_combine_kernel, final_norm=final_norm),
        grid=(n // tm,),
        in_specs=[
            pl.BlockSpec((tm, D_MODEL), row),
            pl.BlockSpec((tm, LANES), row),
            pl.BlockSpec((tm, ROW_SLABS, LANES), lambda i: (i, 0, 0)),
            pl.BlockSpec((tm, ROW_SLABS, LANES), lambda i: (n // tm + i, 0, 0)),
            pl.BlockSpec((tm, D_PLE), row),
            pl.BlockSpec((1, D_MODEL), const),
            pl.BlockSpec((D_MODEL, D_MODEL), const),
            pl.BlockSpec((D_PLE, D_MODEL), const),
            pl.BlockSpec((1, D_MODEL), const),
        ],
        out_specs=pl.BlockSpec((tm, D_MODEL), row),
        out_shape=jax.ShapeDtypeStruct((n, D_MODEL), F32),
        compiler_params=pltpu.CompilerParams(
            dimension_semantics=("arbitrary",), vmem_limit_bytes=_vmem_limit(40 << 20)),
        name="expert_combine_tail",
    )(h1, rf, ys, ys, p, g_ple, w_pg, w_pp, g_final)


def _block_diag(w):
    nb, db, _ = w.shape
    eye = jnp.eye(nb, dtype=w.dtype)
    return (eye[:, None, :, None] * w[:, :, None, :]).reshape(nb * db, nb * db)


def _mixer_layer(h, g_mix, w_in, b_in, w_conv_qk, b_conv_qk, g_mh, w_conv_r, b_conv_r,
                 w_ra, b_ra, w_ri, b_ri, lam, g_r, batch, seq, to_bf16):
    q_end, k_end, v_end, o_end = D_M, 2 * D_M, 3 * D_M, 4 * D_M
    i_end = o_end + NH_M
    f_end = i_end + NH_M
    xr_end = f_end + D_R
    w_main = jnp.concatenate([w_in[:, :o_end], w_in[:, f_end:]], axis=1).astype(BF16)
    b_main = jnp.concatenate([b_in[:o_end], b_in[f_end:]])[None, :]
    w_gt = w_in[:, o_end:f_end].T.astype(BF16)
    b_gt = b_in[o_end:f_end][:, None]
    del q_end, k_end, v_end, xr_end
    wbd = jnp.concatenate([_block_diag(w_ra), _block_diag(w_ri)], axis=1).astype(BF16)
    bbd = jnp.concatenate([b_ra, b_ri])[None, :]
    ones_bd = _block_diag(jnp.ones((NB_R, DB_R, DB_R), F32)).astype(BF16)
    q, k, v, og, a, u, gy, gt = _norm_inproj(
        h, g_mix[None, :], w_main, b_main, w_gt, b_gt, w_conv_qk, b_conv_qk[None, :],
        w_conv_r, b_conv_r[None, :], wbd, bbd, lam[None, :], seq)
    return _mixer(q, k, v, og, a, u, gy, gt, g_mh[None, :], g_r[None, :], ones_bd, batch, seq, to_bf16)


def _expert_tiles(top_idx, counts_f, n_tok):
    tmf = FFN_ROW_TILE
    counts = counts_f.astype(jnp.int32)
    padded = (counts + tmf - 1) // tmf * tmf
    n_tiles = (jnp.sum(padded) // tmf).astype(jnp.int32).reshape(1)
    a_ids = jnp.arange(TOP_K * n_tok, dtype=jnp.int32)
    real = top_idx.reshape(-1) * KEY_STRIDE + a_ids
    pad_j = jnp.arange(tmf, dtype=jnp.int32)[None, :]
    experts = jnp.arange(N_EXPERTS, dtype=jnp.int32)[:, None]
    pad_keys = jnp.where(pad_j < (padded - counts)[:, None],
                         experts * KEY_STRIDE + TOP_K * n_tok + pad_j, N_EXPERTS * KEY_STRIDE)
    keys = jnp.sort(jnp.concatenate([real, pad_keys.reshape(-1)]))
    t_total = (TOP_K * n_tok) // tmf + N_EXPERTS
    a_sorted = keys & (KEY_STRIDE - 1)
    valid = a_sorted < TOP_K * n_tok
    row = jnp.arange(t_total * tmf, dtype=jnp.int32)
    dump = TOP_K * n_tok + (row // tmf % 2) * tmf + row % tmf
    src = jnp.where(valid, a_sorted >> 1, 0).reshape(t_total, 1, tmf)
    dst = jnp.where(valid, (a_sorted & 1) * n_tok + (a_sorted >> 1), dump)
    dst = jnp.concatenate([dst, TOP_K * n_tok + 2 * tmf + row[:tmf]]).reshape(t_total + 1, 1, tmf)
    tile_expert = jnp.minimum(keys[::tmf] // KEY_STRIDE, N_EXPERTS - 1).astype(jnp.int32)
    return src, dst, tile_expert, n_tiles


def kernel(x, p, g_mix, w_in, b_in, w_conv_qk, b_conv_qk, g_mh, w_conv_r, b_conv_r, w_ra, b_ra, w_ri, b_ri, lam, g_r, w_out, g_ffn, w_ff_gate, w_ff_up, w_ff_down, w_router, w_e_gate, w_e_up, w_e_down, g_ple, w_ple_gate, w_ple_proj, g_final):
    batch, seq, _ = x.shape
    n = batch * seq
    depth = g_mix.shape[0]
    h = x.reshape(n, D_MODEL)
    tmf = FFN_ROW_TILE
    for i in range(depth):
        last = i == depth - 1
        j = i // 2
        dense = i % 2 == 0
        if dense:
            ffn_w = [w_ff_gate[j], w_ff_up[j], w_ff_down[j]]
        else:
            ffn_w = [w_e_gate[j].reshape(N_EXPERTS * D_MODEL, D_FF), w_e_up[j].reshape(N_EXPERTS * D_MODEL, D_FF),
                     w_e_down[j].reshape(N_EXPERTS * D_FF, D_MODEL)]
        hcat, (w_o, w_pg, w_pp, w_g, w_u, w_d) = _mixer_layer(
            h, g_mix[i], w_in[i], b_in[i], w_conv_qk[i], b_conv_qk[i], g_mh[i], w_conv_r[i], b_conv_r[i],
            w_ra[i], b_ra[i], w_ri[i], b_ri[i], lam[i], g_r[i], batch, seq,
            [w_out[i], w_ple_gate[i], w_ple_proj[i]] + ffn_w)
        p_i = p[i].reshape(n, D_PLE)
        if dense:
            h = _dense_layer(h, hcat, p_i, w_o, g_ffn[i][None, :], w_g, w_u, w_d, g_ple[i][None, :],
                             w_pg, w_pp, g_final[None, :], last)
        else:
            w_r = jnp.zeros((D_MODEL, LANES), F32).at[:, :N_EXPERTS].set(w_router[j])
            h1, hn, ri, rf, cnt = _outproj_router(h, hcat, w_o, g_ffn[i][None, :], w_r)
            src, dst, tile_expert, n_tiles = _expert_tiles(ri[:, :TOP_K], cnt[0, :N_EXPERTS], n)
            ys = _moe_ffn(hn, w_g.reshape(N_EXPERTS, D_MODEL, D_FF), w_u.reshape(N_EXPERTS, D_MODEL, D_FF),
                          w_d.reshape(N_EXPERTS, D_FF, D_MODEL), src, dst, tile_expert, n_tiles)
            h = _combine(h1, rf, ys, p_i, g_ple[i][None, :], w_pg, w_pp, g_final[None, :], last)
    return h.reshape(batch, seq, D_MODEL)
```
